```python
import jax, jax.numpy as jnp
from jax import lax
import numpy as np

D_MODEL = 1024
BATCH = 32
SEQ = 2048
DEPTH = 2
DEC_BATCH = 32
DEC_SEQ = 64
PAST_LEN = 4096

CHUNK = 64
D_MIX = D_MODEL
CONV_W = D_MIX // 2
CONV_GROUPS = 8
CONV_K = 3
ML_W = D_MIX - CONV_W
ML_H = 4
ML_DK = ML_W // ML_H
ML_DV = ML_W // ML_H
N_IN = 3 * CONV_W + 4 * ML_W + 2 * ML_H
N_EXPERTS = 64
TOP_K = 8
N_GROUPS = 8
TOPK_GROUPS = 4
D_EXPERT = D_MODEL // 4
D_SHARED = D_MODEL // 4
ROUTED_SCALE = 2.5
MOE_BLOCK = 128
EPS = 1e-6

kernel_name = 'hybrid_conv_mlstm_moe_stream_step'


def rmsnorm(x, g):
    xf = x.astype(jnp.float32)
    y = xf * lax.rsqrt(jnp.mean(xf * xf, axis=-1, keepdims=True) + EPS)
    return (y * g.astype(jnp.float32)).astype(x.dtype)


def short_conv(cb, cc, ch, conv_w, conv_prev):
    u = cc * ch
    up = jnp.concatenate([conv_prev.astype(u.dtype), u], axis=1)
    S = u.shape[1]
    y = sum(conv_w[j] * up[:, j:j + S] for j in range(CONV_K))
    return cb * y, up[:, -(CONV_K - 1):]


def mlstm_chunk(carry, inp):
    C, n, m = carry
    q, k, v, ig, lf = inp
    L = q.shape[2]
    F = jnp.cumsum(lf, axis=-1)
    dmat = F[..., :, None] - F[..., None, :] + ig[..., None, :]
    dmat = jnp.where(jnp.tril(jnp.ones((L, L), bool)), dmat, -jnp.inf)
    prior = F + m[..., None]
    m_t = jnp.maximum(prior, jnp.max(dmat, axis=-1))
    w_prior = jnp.exp(prior - m_t)
    s = jnp.einsum('bhtk,bhsk->bhts', q, k) * jnp.exp(dmat - m_t[..., None])
    num = jnp.einsum('bhts,bhsv->bhtv', s, v) + w_prior[..., None] * jnp.einsum('bhtk,bhkv->bhtv', q, C)
    den = jnp.sum(s, axis=-1) + w_prior * jnp.einsum('bhtk,bhk->bht', q, n)
    h = num / jnp.maximum(jnp.abs(den), jnp.exp(-m_t))[..., None]
    m_new = m_t[..., -1]
    F_L = F[..., -1]
    a = jnp.exp(F_L + m - m_new)
    ws = jnp.exp(F_L[..., None] - F + ig - m_new[..., None])
    C_new = a[..., None, None] * C + jnp.einsum('bhs,bhsk,bhsv->bhkv', ws, k, v)
    n_new = a[..., None] * n + jnp.einsum('bhs,bhsk->bhk', ws, k)
    return (C_new, n_new, m_new), h


def mlstm(q, k, v, ig, lf, C0, n0, m0):
    B, S, H, _ = q.shape
    L = min(S, CHUNK)
    NC = S // L
    def blocks4(t):
        return t.reshape(B, NC, L, H, t.shape[-1]).transpose(1, 0, 3, 2, 4)
    def blocks3(t):
        return t.reshape(B, NC, L, H).transpose(1, 0, 3, 2)
    carry0 = (C0.astype(jnp.float32), n0.astype(jnp.float32), m0.astype(jnp.float32))
    (C, n, m), hs = lax.scan(mlstm_chunk, carry0,
                             (blocks4(q), blocks4(k), blocks4(v), blocks3(ig), blocks3(lf)))
    h = hs.transpose(1, 0, 3, 2, 4).reshape(B, S, H, ML_DV)
    return h, C, n, m


def routed_experts(xf, idx, wts, w1, w3, w2):
    T, D = xf.shape
    A = T * TOP_K
    flat_e = idx.reshape(-1).astype(jnp.int32)
    flat_w = wts.reshape(-1).astype(jnp.float32)
    order = jnp.argsort(flat_e)
    sorted_e = flat_e[order]
    counts = jnp.bincount(flat_e, length=N_EXPERTS)
    starts = jnp.cumsum(counts) - counts
    padded = (counts + MOE_BLOCK - 1) // MOE_BLOCK * MOE_BLOCK
    pends = jnp.cumsum(padded)
    pstarts = pends - padded
    dest = pstarts[sorted_e] + (jnp.arange(A, dtype=jnp.int32) - starts[sorted_e])
    NB = -(-A // MOE_BLOCK) + N_EXPERTS
    P = NB * MOE_BLOCK
    row_tok = jnp.zeros((P,), jnp.int32).at[dest].set((order // TOP_K).astype(jnp.int32))
    row_w = jnp.zeros((P,), jnp.float32).at[dest].set(flat_w[order])
    block_e = jnp.minimum(jnp.searchsorted(pends, jnp.arange(NB, dtype=jnp.int32) * MOE_BLOCK, side='right'),
                          N_EXPERTS - 1).astype(jnp.int32)

    def body(acc, blk):
        rows, rw, e = blk
        xb = xf[rows]
        yb = (jax.nn.silu(xb @ w1[e]) * (xb @ w3[e])) @ w2[e]
        return acc.at[rows].add(yb.astype(jnp.float32) * rw[:, None]), None

    acc, _ = lax.scan(body, jnp.zeros((T, D), jnp.float32),
                      (row_tok.reshape(NB, MOE_BLOCK), row_w.reshape(NB, MOE_BLOCK), block_e))
    return acc


def moe(h, router_w, router_b, w1, w3, w2, sw1, sw3, sw2):
    B, S, D = h.shape
    xf = h.reshape(-1, D)
    T = xf.shape[0]
    scores = jax.nn.sigmoid((xf @ router_w).astype(jnp.float32))
    sel = scores + router_b.astype(jnp.float32)
    grp = lax.top_k(sel.reshape(T, N_GROUPS, N_EXPERTS // N_GROUPS), 2)[0].sum(-1)
    _, gidx = lax.top_k(grp, TOPK_GROUPS)
    gmask = jnp.sum(jax.nn.one_hot(gidx, N_GROUPS, dtype=jnp.float32), axis=1) > 0
    emask = jnp.repeat(gmask, N_EXPERTS // N_GROUPS, axis=1)
    _, idx = lax.top_k(jnp.where(emask, sel, -jnp.inf), TOP_K)
    wts = jnp.take_along_axis(scores, idx, axis=-1)
    wts = wts / jnp.sum(wts, axis=-1, keepdims=True) * ROUTED_SCALE
    routed = routed_experts(xf, idx, wts, w1, w3, w2)
    shared = ((jax.nn.silu(xf @ sw1) * (xf @ sw3)) @ sw2).astype(jnp.float32)
    return (routed + shared).astype(h.dtype).reshape(B, S, D)


def layer(x, c, conv_prev, C0, n0, m0, norm1_g, ada_w, ada_b, w_in, b_igate, b_fgate, conv_w,
          conv_norm_g, mlstm_norm_g, w_out, norm2_g, router_w, router_b, exp_w1, exp_w3, exp_w2,
          shared_w1, shared_w3, shared_w2):
    B, S, _ = x.shape
    mod = jax.nn.silu(c) @ ada_w + ada_b
    sh1, sc1, g1, sh2, sc2, g2 = jnp.split(mod, 6, axis=-1)
    h = rmsnorm(x, norm1_g) * (1 + sc1[:, None]) + sh1[:, None]
    proj = h @ w_in
    o = 3 * CONV_W
    cb, cc, ch, q, k, v, og, ig, fg = jnp.split(
        proj, [CONV_W, 2 * CONV_W, o, o + ML_W, o + 2 * ML_W, o + 3 * ML_W, o + 4 * ML_W,
               o + 4 * ML_W + ML_H], axis=-1)
    conv_out, conv_state = short_conv(cb, cc, ch, conv_w, conv_prev)
    conv_out = rmsnorm(conv_out, conv_norm_g)
    q = q.astype(jnp.float32).reshape(B, S, ML_H, ML_DK)
    k = k.astype(jnp.float32).reshape(B, S, ML_H, ML_DK) * (ML_DK ** -0.5)
    v = v.astype(jnp.float32).reshape(B, S, ML_H, ML_DV)
    ig = (ig + b_igate).astype(jnp.float32)
    lf = jax.nn.log_sigmoid((fg + b_fgate).astype(jnp.float32))
    hm, C, n, m = mlstm(q, k, v, ig, lf, C0, n0, m0)
    hm = rmsnorm(hm, mlstm_norm_g.reshape(ML_H, ML_DV)).reshape(B, S, ML_W)
    hm = (hm * jax.nn.sigmoid(og.astype(jnp.float32))).astype(x.dtype)
    mix = jnp.concatenate([conv_out, hm], axis=-1) @ w_out
    x = x + g1[:, None] * mix
    h2 = rmsnorm(x, norm2_g) * (1 + sc2[:, None]) + sh2[:, None]
    x = x + g2[:, None] * moe(h2, router_w, router_b, exp_w1, exp_w3, exp_w2, shared_w1, shared_w3, shared_w2)
    return x, conv_state.astype(x.dtype), C.astype(x.dtype), n.astype(x.dtype), m.astype(x.dtype)


def trunk(x, c, conv0, C0, n0, m0, params, final_g):
    convs, Cs, ns, ms = [], [], [], []
    for l in range(DEPTH):
        lp = [p[l] for p in params]
        x, cs, C, n, m = layer(x, c, conv0[l], C0[l], n0[l], m0[l], *lp)
        convs.append(cs); Cs.append(C); ns.append(n); ms.append(m)
    return rmsnorm(x, final_g), jnp.stack(convs), jnp.stack(Cs), jnp.stack(ns), jnp.stack(ms)


def setup_inputs(seed: int = 0) -> dict:
    key = jax.random.key(seed)
    ks = jax.random.split(key, 32)
    nrm = jax.random.normal
    f32 = jnp.float32
    L, D, E, F = DEPTH, D_MODEL, N_EXPERTS, D_EXPERT
    return {
        'x_prompt': nrm(ks[0], (BATCH, SEQ, D), f32),
        'x_sample': nrm(ks[1], (DEC_BATCH, DEC_SEQ, D), f32),
        'c_prompt': nrm(ks[2], (BATCH, D), f32),
        'c_sample': nrm(ks[3], (DEC_BATCH, D), f32),
        'cache_conv': nrm(ks[4], (L, DEC_BATCH, CONV_K - 1, CONV_W), f32),
        'state_mlstm_C': 0.3 * nrm(ks[5], (L, DEC_BATCH, ML_H, ML_DK, ML_DV), f32),
        'state_mlstm_n': 0.3 * nrm(ks[6], (L, DEC_BATCH, ML_H, ML_DK), f32),
        'state_mlstm_m': nrm(ks[7], (L, DEC_BATCH, ML_H), f32),
        'norm1_g': 1.0 + 0.02 * nrm(ks[8], (L, D), f32),
        'ada_w': (0.5 * D ** -0.5) * nrm(ks[9], (L, D, 6 * D), f32),
        'ada_b': 0.02 * nrm(ks[10], (L, 6 * D), f32),
        'w_in': (D ** -0.5) * nrm(ks[11], (L, D, N_IN), f32),
        'b_igate': 0.1 * nrm(ks[12], (L, ML_H), f32),
        'b_fgate': 3.0 + 0.5 * nrm(ks[13], (L, ML_H), f32),
        'conv_w': (CONV_K ** -0.5) * nrm(ks[14], (L, CONV_K, CONV_W), f32),
        'conv_norm_g': 1.0 + 0.02 * nrm(ks[15], (L, CONV_W), f32),
        'mlstm_norm_g': 1.0 + 0.02 * nrm(ks[16], (L, ML_W), f32),
        'w_out': (D_MIX ** -0.5) * nrm(ks[17], (L, D_MIX, D), f32),
        'norm2_g': 1.0 + 0.02 * nrm(ks[18], (L, D), f32),
        'router_w': (D ** -0.5) * nrm(ks[19], (L, D, E), f32),
        'router_b': 0.01 * nrm(ks[20], (L, E), f32),
        'exp_w1': (D ** -0.5) * nrm(ks[21], (L, E, D, F), f32),
        'exp_w3': (D ** -0.5) * nrm(ks[22], (L, E, D, F), f32),
        'exp_w2': (F ** -0.5) * nrm(ks[23], (L, E, F, D), f32),
        'shared_w1': (D ** -0.5) * nrm(ks[24], (L, D, D_SHARED), f32),
        'shared_w3': (D ** -0.5) * nrm(ks[25], (L, D, D_SHARED), f32),
        'shared_w2': (D_SHARED ** -0.5) * nrm(ks[26], (L, D_SHARED, D), f32),
        'final_g': 1.0 + 0.02 * nrm(ks[27], (D,), f32),
    }


def reference(x_prompt, x_sample, c_prompt, c_sample, cache_conv, state_mlstm_C, state_mlstm_n,
              state_mlstm_m, norm1_g, ada_w, ada_b, w_in, b_igate, b_fgate, conv_w, conv_norm_g,
              mlstm_norm_g, w_out, norm2_g, router_w, router_b, exp_w1, exp_w3, exp_w2,
              shared_w1, shared_w3, shared_w2, final_g):
    params = (norm1_g, ada_w, ada_b, w_in, b_igate, b_fgate, conv_w, conv_norm_g, mlstm_norm_g,
              w_out, norm2_g, router_w, router_b, exp_w1, exp_w3, exp_w2, shared_w1, shared_w3, shared_w2)
    B = x_prompt.shape[0]
    dt = x_prompt.dtype
    conv0 = jnp.zeros((DEPTH, B, CONV_K - 1, CONV_W), dt)
    C0 = jnp.zeros((DEPTH, B, ML_H, ML_DK, ML_DV), dt)
    n0 = jnp.zeros((DEPTH, B, ML_H, ML_DK), dt)
    m0 = jnp.zeros((DEPTH, B, ML_H), dt)
    y_prompt, conv_p, C_p, n_p, m_p = trunk(x_prompt, c_prompt, conv0, C0, n0, m0, params, final_g)
    y_sample, conv_s, C_s, n_s, m_s = trunk(x_sample, c_sample, cache_conv, state_mlstm_C,
                                            state_mlstm_n, state_mlstm_m, params, final_g)
    return (y_prompt, y_sample, conv_p, C_p, n_p, m_p, conv_s, C_s, n_s, m_s)
```

```python
import functools

import jax
import jax.numpy as jnp
from jax import lax
from jax.experimental import pallas as pl
from jax.experimental.pallas import tpu as pltpu

F32 = jnp.float32
BF16 = jnp.bfloat16
I32 = jnp.int32
HIGHEST = lax.Precision.HIGHEST

D_MODEL = 1024
CONV_W = 512
CONV_K = 3
ML_W = 512
ML_H = 4
ML_DK = 128
CHUNK = 64
N_EXPERTS = 64
TOP_K = 8
N_GROUPS = 8
GROUP_SIZE = N_EXPERTS // N_GROUPS
TOPK_GROUPS = 4
D_EXPERT = 256
ROUTED_SCALE = 2.5
EPS = 1e-6
K_SCALE = ML_DK ** -0.5
N_MAIN = 3 * CONV_W + 4 * ML_W
N_GATE = 2 * ML_H
LANES = 128
ROW_TILE = 512
VMEM_LIMIT = 56 * 1024 * 1024

NT_DIMS = (((1,), (1,)), ((), ()))


def _cparams(*sem):
    return pltpu.CompilerParams(dimension_semantics=sem, vmem_limit_bytes=VMEM_LIMIT)


def _rms(x):
    return x * lax.rsqrt(jnp.mean(x * x, axis=-1, keepdims=True) + EPS)


def _log_sigmoid(x):
    return jnp.minimum(x, 0.0) - jnp.log1p(jnp.exp(-jnp.abs(x)))


def _seq_tile(batch, seq, rows):
    if seq >= rows:
        return 1, rows
    return min(batch, rows // seq), seq


def _ada_kernel(c_ref, w_ref, b_ref, o_ref):
    c = c_ref[...]
    s = (c * jax.nn.sigmoid(c)).astype(BF16)
    o_ref[0] = jnp.dot(s, w_ref[0].astype(BF16), preferred_element_type=F32) + b_ref[0]


def _ada(c_all, ada_w, ada_b):
    depth, d, n = ada_w.shape
    nb = c_all.shape[0]
    tn = 1536
    return pl.pallas_call(
        _ada_kernel,
        grid=(depth, n // tn),
        in_specs=[
            pl.BlockSpec((nb, d), lambda l, j: (0, 0)),
            pl.BlockSpec((1, d, tn), lambda l, j: (l, 0, j)),
            pl.BlockSpec((1, 1, tn), lambda l, j: (l, 0, j)),
        ],
        out_specs=pl.BlockSpec((1, nb, tn), lambda l, j: (l, 0, j)),
        out_shape=jax.ShapeDtypeStruct((depth, nb, n), F32),
        compiler_params=_cparams("parallel", "parallel"),
        name="ada",
    )(c_all, ada_w, ada_b.reshape(depth, 1, n))


def _inproj_kernel(x_ref, mod_ref, g_ref, wm_ref, wgc_ref, wgr_ref, bc_ref, br_ref,
                   pc_ref, pm_ref, gc_ref, gr_ref):
    nb, ts, d = x_ref.shape
    m = nb * ts
    sh = mod_ref[:, :, 0:d]
    sc = mod_ref[:, :, d:2 * d]
    h = (_rms(x_ref[...]) * g_ref[...] * (1.0 + sc) + sh).reshape(m, d)
    hb = h.astype(BF16)
    cn = 512
    for j in range(N_MAIN // cn):
        pj = jnp.dot(hb, wm_ref[:, j * cn:(j + 1) * cn], preferred_element_type=F32).astype(BF16)
        if j < 3:
            pc_ref[:, :, j * cn:(j + 1) * cn] = pj.reshape(nb, ts, cn)
        else:
            pm_ref[:, :, (j - 3) * cn:(j - 2) * cn] = pj.reshape(nb, ts, cn)
    gcol = jnp.dot(h, wgc_ref[...], precision=HIGHEST, preferred_element_type=F32) + bc_ref[...]
    lane = lax.broadcasted_iota(I32, gcol.shape, 1)
    gc_ref[...] = jnp.where(lane >= ML_H, _log_sigmoid(gcol), gcol).reshape(nb, ts, LANES)
    grow = lax.dot_general(wgr_ref[...], h, NT_DIMS, precision=HIGHEST,
                           preferred_element_type=F32) + br_ref[...]
    row = lax.broadcasted_iota(I32, grow.shape, 0)
    grow = jnp.where(row >= ML_H, _log_sigmoid(grow), grow)
    for b in range(nb):
        gr_ref[b] = grow[:, b * ts:(b + 1) * ts]


def _inproj(x, mod, norm_g, w_main, wg_col, wg_row, b_col, b_row):
    batch, seq, d = x.shape
    nb, ts = _seq_tile(batch, seq, ROW_TILE)
    grid = (batch // nb, seq // ts)
    tile = lambda w: pl.BlockSpec((nb, ts, w), lambda bi, si: (bi, si, 0))
    full = lambda a: pl.BlockSpec(a.shape, lambda bi, si: (0,) * a.ndim)
    return pl.pallas_call(
        _inproj_kernel,
        grid=grid,
        in_specs=[tile(d), pl.BlockSpec((nb, 1, 6 * d), lambda bi, si: (bi, 0, 0)),
                  full(norm_g), full(w_main), full(wg_col), full(wg_row), full(b_col), full(b_row)],
        out_specs=[tile(3 * CONV_W), tile(4 * ML_W), tile(LANES),
                   pl.BlockSpec((nb, N_GATE, ts), lambda bi, si: (bi, 0, si))],
        out_shape=[jax.ShapeDtypeStruct((batch, seq, 3 * CONV_W), BF16),
                   jax.ShapeDtypeStruct((batch, seq, 4 * ML_W), BF16),
                   jax.ShapeDtypeStruct((batch, seq, LANES), F32),
                   jax.ShapeDtypeStruct((batch, N_GATE, seq), F32)],
        compiler_params=_cparams("parallel", "parallel"),
        name="inproj",
    )(x, mod, norm_g, w_main, wg_col, wg_row, b_col, b_row)


def _mixer_kernel(x_ref, mod_ref, pc_ref, pm_ref, gc_ref, gr_ref, conv0_ref, c0_ref, n0_ref, m0_ref,
                  convw_ref, cng_ref, mng_ref, wout_ref, n2g_ref, rwt_ref,
                  xo_ref, h2_ref, lt_ref, convo_ref, co_ref, no_ref, mo_ref,
                  ubuf, c_s, n_s, m_s, hm_s):
    nb, ts, d = x_ref.shape
    nch = ts // CHUNK
    m = nb * ts
    si = pl.program_id(1)
    hdr = 8

    @pl.when(si == 0)
    def _load_state():
        ubuf[:, hdr - 2:hdr, :] = conv0_ref[...]
        c_s[...] = c0_ref[...]
        n_s[...] = n0_ref[...]
        m_s[...] = m0_ref[...]

    cb = pc_ref[:, :, 0:CONV_W].astype(F32)
    cc = pc_ref[:, :, CONV_W:2 * CONV_W].astype(F32)
    ch = pc_ref[:, :, 2 * CONV_W:3 * CONV_W].astype(F32)
    ubuf[:, hdr:hdr + ts, :] = cc * ch
    w = convw_ref[...]
    y = (w[0:1] * ubuf[:, hdr - 2:hdr - 2 + ts, :] + w[1:2] * ubuf[:, hdr - 1:hdr - 1 + ts, :]
         + w[2:3] * ubuf[:, hdr:hdr + ts, :])
    co = (_rms(cb * y) * cng_ref[...]).reshape(m, CONV_W).astype(BF16)
    last2 = ubuf[:, hdr + ts - 2:hdr + ts, :]
    ubuf[:, hdr - 2:hdr, :] = last2

    r_i = lax.broadcasted_iota(I32, (CHUNK, CHUNK), 0)
    c_i = lax.broadcasted_iota(I32, (CHUNK, CHUNK), 1)
    causal = r_i >= c_i
    ltri = causal.astype(F32)
    utri = (r_i <= c_i).astype(F32)

    def unit(u, carry):
        if nch == 1:
            b, c = u, 0
        elif nb == 1:
            b, c = 0, u
        else:
            b, c = u // nch, u % nch
        t0 = pl.multiple_of(c * CHUNK, CHUNK)
        rows = pl.ds(t0, CHUNK)
        gcol = gc_ref[b, rows, :]
        grow = gr_ref[b, c]
        f_col = jnp.dot(ltri, gcol, precision=HIGHEST, preferred_element_type=F32)
        f_row = jnp.dot(grow, utri, precision=HIGHEST, preferred_element_type=F32)
        for h in range(ML_H):
            q = pm_ref[b, rows, h * ML_DK:(h + 1) * ML_DK]
            k = pm_ref[b, rows, ML_W + h * ML_DK:ML_W + (h + 1) * ML_DK]
            v = pm_ref[b, rows, 2 * ML_W + h * ML_DK:2 * ML_W + (h + 1) * ML_DK]
            og = pm_ref[b, rows, 3 * ML_W + h * ML_DK:3 * ML_W + (h + 1) * ML_DK]
            fc = f_col[:, ML_H + h:ML_H + h + 1]
            fr = f_row[ML_H + h:ML_H + h + 1, :]
            igc = gcol[:, h:h + 1]
            igr = grow[h:h + 1, :]
            dmat = jnp.where(causal, fc - fr + igr, -jnp.inf)
            m_prev = m_s[b, h:h + 1, 0:1]
            prior = fc + m_prev
            m_t = jnp.maximum(prior, jnp.max(dmat, axis=1, keepdims=True))
            w_prior = jnp.exp(prior - m_t)
            p = jnp.exp(dmat - m_t)
            qk = lax.dot_general(q, k, NT_DIMS, preferred_element_type=F32)
            s = qk * (p * K_SCALE)
            c_old = c_s[b, h]
            n_old = n_s[b, h:h + 1, :]
            num = (jnp.dot(s.astype(BF16), v, preferred_element_type=F32)
                   + w_prior * jnp.dot(q, c_old.astype(BF16), preferred_element_type=F32))
            den = (jnp.sum(s, axis=1, keepdims=True)
                   + w_prior * jnp.sum(q.astype(F32) * n_old, axis=1, keepdims=True))
            hh = num / jnp.maximum(jnp.abs(den), jnp.exp(-m_t))
            m_new = m_t[CHUNK - 1:CHUNK, :]
            f_last = fc[CHUNK - 1:CHUNK, :]
            a = jnp.exp(f_last + m_prev - m_new)
            ws = jnp.exp(f_last - fc + igc - m_new) * K_SCALE
            kw = k.astype(F32) * ws
            c_s[b, h] = a * c_old + jnp.dot(kw.T.astype(BF16), v, preferred_element_type=F32)
            n_s[b, h:h + 1, :] = a * n_old + jnp.sum(kw, axis=0, keepdims=True)
            m_s[b, h:h + 1, :] = jnp.broadcast_to(m_new, (1, LANES))
            hn = _rms(hh) * mng_ref[:, h * ML_DK:(h + 1) * ML_DK]
            hn = hn * jax.nn.sigmoid(og.astype(F32))
            hm_s[pl.ds(pl.multiple_of(u * CHUNK, CHUNK), CHUNK), h * ML_DK:(h + 1) * ML_DK] = hn
        return carry

    lax.fori_loop(0, nb * nch, unit, 0)

    mix = (jnp.dot(co, wout_ref[0:CONV_W, :], preferred_element_type=F32)
           + jnp.dot(hm_s[...].astype(BF16), wout_ref[CONV_W:CONV_W + ML_W, :], preferred_element_type=F32))
    g1 = mod_ref[:, :, 2 * d:3 * d]
    sh2 = mod_ref[:, :, 3 * d:4 * d]
    sc2 = mod_ref[:, :, 4 * d:5 * d]
    xn = x_ref[...] + g1 * mix.reshape(nb, ts, d)
    xo_ref[...] = xn
    h2 = _rms(xn) * n2g_ref[...] * (1.0 + sc2) + sh2
    h2_ref[...] = h2
    lt_ref[...] = lax.dot_general(rwt_ref[...], h2.reshape(m, d), NT_DIMS, precision=HIGHEST,
                                  preferred_element_type=F32)

    @pl.when(si == pl.num_programs(1) - 1)
    def _store_state():
        convo_ref[...] = last2
        co_ref[...] = c_s[...]
        no_ref[...] = n_s[...]
        mo_ref[...] = m_s[...]


def _mixer(x, mod, pc, pm, gcol, grow, conv0, c0, n0, m0, conv_w, cn_g, mn_g, w_out, n2_g, rw_t):
    batch, seq, d = x.shape
    nb, ts = _seq_tile(batch, seq, ROW_TILE)
    nch = ts // CHUNK
    n_si = seq // ts
    grid = (batch // nb, n_si)
    tile = lambda w: pl.BlockSpec((nb, ts, w), lambda bi, si: (bi, si, 0))
    full = lambda a: pl.BlockSpec(a.shape, lambda bi, si: (0,) * a.ndim)
    per_b = lambda *tail: pl.BlockSpec((nb,) + tail, lambda bi, si: (bi,) + (0,) * len(tail))
    tokens = batch * seq
    return pl.pallas_call(
        _mixer_kernel,
        grid=grid,
        in_specs=[tile(d), per_b(1, 6 * d), tile(3 * CONV_W), tile(4 * ML_W), tile(LANES),
                  pl.BlockSpec((nb, nch, N_GATE, CHUNK), lambda bi, si: (bi, si, 0, 0)),
                  per_b(CONV_K - 1, CONV_W), per_b(ML_H, ML_DK, ML_DK), per_b(8, LANES), per_b(8, LANES),
                  full(conv_w), full(cn_g), full(mn_g), full(w_out), full(n2_g), full(rw_t)],
        out_specs=[tile(d), tile(d),
                   pl.BlockSpec((N_EXPERTS, nb * ts), lambda bi, si: (0, bi * n_si + si)),
                   per_b(CONV_K - 1, CONV_W), per_b(ML_H, ML_DK, ML_DK), per_b(8, LANES), per_b(8, LANES)],
        out_shape=[jax.ShapeDtypeStruct((batch, seq, d), F32),
                   jax.ShapeDtypeStruct((batch, seq, d), F32),
                   jax.ShapeDtypeStruct((N_EXPERTS, tokens), F32),
                   jax.ShapeDtypeStruct((batch, CONV_K - 1, CONV_W), F32),
                   jax.ShapeDtypeStruct((batch, ML_H, ML_DK, ML_DK), F32),
                   jax.ShapeDtypeStruct((batch, 8, LANES), F32),
                   jax.ShapeDtypeStruct((batch, 8, LANES), F32)],
        scratch_shapes=[pltpu.VMEM((nb, ts + 8, CONV_W), F32),
                        pltpu.VMEM((nb, ML_H, ML_DK, ML_DK), F32),
                        pltpu.VMEM((nb, 8, LANES), F32),
                        pltpu.VMEM((nb, 8, LANES), F32),
                        pltpu.VMEM((nb * ts, ML_W), F32)],
        compiler_params=_cparams("parallel", "arbitrary"),
        name="mixer",
    )(x, mod, pc, pm, gcol, grow, conv0, c0, n0, m0, conv_w, cn_g, mn_g, w_out, n2_g, rw_t)


def _route_kernel(lt_ref, rb_ref, tri_ref, idx_ref, wts_ref, pos_ref, cnt_ref, cnt_s):
    tt = lt_ref.shape[1]

    @pl.when(pl.program_id(0) == 0)
    def _zero():
        cnt_s[...] = jnp.zeros_like(cnt_s)

    neg = -jnp.inf
    scores = jax.nn.sigmoid(lt_ref[...])
    sel3 = (scores + rb_ref[...]).reshape(N_GROUPS, GROUP_SIZE, tt)
    sc3 = scores.reshape(N_GROUPS, GROUP_SIZE, tt)
    shape3 = (N_GROUPS, GROUP_SIZE, tt)
    j3 = lax.broadcasted_iota(I32, shape3, 1).astype(F32)
    g3 = lax.broadcasted_iota(I32, shape3, 0).astype(F32)
    e3 = g3 * GROUP_SIZE + j3
    m1 = jnp.max(sel3, axis=1, keepdims=True)
    first = jnp.min(jnp.where(sel3 == m1, j3, float(GROUP_SIZE)), axis=1, keepdims=True)
    m2 = jnp.max(jnp.where(j3 == first, neg, sel3), axis=1, keepdims=True)
    grp = m1 + m2
    gi = lax.broadcasted_iota(I32, grp.shape, 0).astype(F32)
    gmask = jnp.zeros(grp.shape, F32)
    for _ in range(TOPK_GROUPS):
        gmax = jnp.max(grp, axis=0, keepdims=True)
        gidx = jnp.min(jnp.where(grp == gmax, gi, float(N_GROUPS)), axis=0, keepdims=True)
        hit = gi == gidx
        gmask = jnp.where(hit, 1.0, gmask)
        grp = jnp.where(hit, neg, grp)
    cand = jnp.where(gmask > 0.0, sel3, neg)

    def red(fn, x):
        return fn(fn(x, axis=1, keepdims=True), axis=0, keepdims=True)

    selm = jnp.zeros(shape3, F32)
    idxs, ws = [], []
    for _ in range(TOP_K):
        cmax = red(jnp.max, cand)
        eidx = red(jnp.min, jnp.where(cand == cmax, e3, float(N_EXPERTS)))
        hit = e3 == eidx
        idxs.append(eidx)
        ws.append(red(jnp.sum, jnp.where(hit, sc3, 0.0)))
        selm = jnp.where(hit, 1.0, selm)
        cand = jnp.where(hit, neg, cand)
    wsum = ws[0]
    for k in range(1, TOP_K):
        wsum = wsum + ws[k]
    cum = jnp.dot(selm.reshape(N_EXPERTS, tt).astype(BF16), tri_ref[...],
                  preferred_element_type=F32) + cnt_s[:, 0:1]
    cum3 = cum.reshape(shape3)
    for k in range(TOP_K):
        hit = e3 == idxs[k]
        idx_ref[k:k + 1, :] = idxs[k].reshape(1, tt).astype(I32)
        wts_ref[k:k + 1, :] = (ws[k] / wsum * ROUTED_SCALE).reshape(1, tt)
        pos_ref[k:k + 1, :] = red(jnp.sum, jnp.where(hit, cum3, 0.0)).reshape(1, tt).astype(I32)
    cnt_s[...] = cnt_s[...] + jnp.sum(selm.reshape(N_EXPERTS, tt), axis=1, keepdims=True)
    cnt_ref[...] = cnt_s[...]


def _route(logits_t, router_b):
    tokens = logits_t.shape[1]
    tt = min(ROW_TILE, tokens)
    r_i = lax.broadcasted_iota(I32, (tt, tt), 0)
    c_i = lax.broadcasted_iota(I32, (tt, tt), 1)
    tri = (r_i < c_i).astype(BF16)
    row = lambda: pl.BlockSpec((TOP_K, tt), lambda i: (0, i))
    return pl.pallas_call(
        _route_kernel,
        grid=(tokens // tt,),
        in_specs=[pl.BlockSpec((N_EXPERTS, tt), lambda i: (0, i)),
                  pl.BlockSpec((N_EXPERTS, 1), lambda i: (0, 0)),
                  pl.BlockSpec((tt, tt), lambda i: (0, 0))],
        out_specs=[row(), row(), row(), pl.BlockSpec((N_EXPERTS, LANES), lambda i: (0, 0))],
        out_shape=[jax.ShapeDtypeStruct((TOP_K, tokens), I32),
                   jax.ShapeDtypeStruct((TOP_K, tokens), F32),
                   jax.ShapeDtypeStruct((TOP_K, tokens), I32),
                   jax.ShapeDtypeStruct((N_EXPERTS, LANES), F32)],
        scratch_shapes=[pltpu.VMEM((N_EXPERTS, LANES), F32)],
        compiler_params=_cparams("arbitrary"),
        name="route",
    )(logits_t, router_b.reshape(N_EXPERTS, 1), tri)


def _dispatch_kernel(dest_ref, h_hbm, xs_in, xs_hbm, sem):
    del xs_in
    td = dest_ref.shape[1]
    base = pl.program_id(0) * td

    def issue(t, carry):
        for k in range(TOP_K):
            pltpu.make_async_copy(h_hbm.at[pl.ds(base + t, 1)],
                                  xs_hbm.at[pl.ds(dest_ref[k, t], 1)], sem).start()
        return carry

    lax.fori_loop(0, td, issue, 0)
    for _ in range(TOP_K):
        pltpu.make_async_copy(h_hbm.at[pl.ds(0, td)], xs_hbm.at[pl.ds(0, td)], sem).wait()


def _dispatch(dest, h2_flat, rows_total):
    tokens, d = h2_flat.shape
    td = min(ROW_TILE, tokens)
    zeros = jnp.zeros((rows_total, d), h2_flat.dtype)
    return pl.pallas_call(
        _dispatch_kernel,
        grid=(tokens // td,),
        in_specs=[pl.BlockSpec((TOP_K, td), lambda i: (0, i), memory_space=pltpu.SMEM),
                  pl.BlockSpec(memory_space=pl.ANY),
                  pl.BlockSpec(memory_space=pl.ANY)],
        out_specs=pl.BlockSpec(memory_space=pl.ANY),
        out_shape=jax.ShapeDtypeStruct((rows_total, d), h2_flat.dtype),
        scratch_shapes=[pltpu.SemaphoreType.DMA],
        input_output_aliases={2: 0},
        compiler_params=_cparams("arbitrary"),
        name="dispatch",
    )(dest, h2_flat, zeros)


def _expert_kernel(be_ref, nu_ref, x_ref, w1_ref, w3_ref, w2_ref, y_ref):
    del be_ref
    j = pl.program_id(0)

    @pl.when(j < nu_ref[0])
    def _compute():
        xb = x_ref[...].astype(BF16)
        a = jnp.dot(xb, w1_ref[0], preferred_element_type=F32)
        b = jnp.dot(xb, w3_ref[0], preferred_element_type=F32)
        hmid = (a * jax.nn.sigmoid(a) * b).astype(BF16)
        y_ref[...] = jnp.dot(hmid, w2_ref[0], preferred_element_type=F32)

    @pl.when(j >= nu_ref[0])
    def _unused():
        y_ref[...] = jnp.zeros_like(y_ref)


def _experts(block_e, n_used, xs, w1, w3, w2, rb):
    rows, d = xs.shape
    nblk = rows // rb
    f = w1.shape[2]
    return pl.pallas_call(
        _expert_kernel,
        grid_spec=pltpu.PrefetchScalarGridSpec(
            num_scalar_prefetch=2,
            grid=(nblk,),
            in_specs=[pl.BlockSpec((rb, d), lambda j, be, nu: (jnp.minimum(j, nu[0] - 1), 0)),
                      pl.BlockSpec((1, d, f), lambda j, be, nu: (be[j], 0, 0)),
                      pl.BlockSpec((1, d, f), lambda j, be, nu: (be[j], 0, 0)),
                      pl.BlockSpec((1, f, d), lambda j, be, nu: (be[j], 0, 0))],
            out_specs=pl.BlockSpec((rb, d), lambda j, be, nu: (j, 0)),
        ),
        out_shape=jax.ShapeDtypeStruct((rows, d), F32),
        compiler_params=_cparams("arbitrary"),
        name="experts",
    )(block_e, n_used, xs, w1, w3, w2)


def _combine_kernel(dest_ref, y_hbm, wts_ref, h2_ref, x_ref, mod_ref, sw1_ref, sw3_ref, sw2_ref, fg_ref,
                    out_ref, buf, sem, *, final):
    nb, ts, d = x_ref.shape
    tc = nb * ts

    def issue(t, carry):
        for k in range(TOP_K):
            pltpu.make_async_copy(y_hbm.at[pl.ds(dest_ref[k, t], 1)],
                                  buf.at[k, pl.ds(t, 1)], sem).start()
        return carry

    lax.fori_loop(0, tc, issue, 0)

    hb = h2_ref[...].reshape(tc, d).astype(BF16)
    a = jnp.dot(hb, sw1_ref[...], preferred_element_type=F32)
    b = jnp.dot(hb, sw3_ref[...], preferred_element_type=F32)
    acc = jnp.dot((a * jax.nn.sigmoid(a) * b).astype(BF16), sw2_ref[...], preferred_element_type=F32)

    for k in range(TOP_K):
        pltpu.make_async_copy(y_hbm.at[pl.ds(0, tc)], buf.at[k], sem).wait()
    wts = wts_ref[...]
    for k in range(TOP_K):
        acc = acc + buf[k] * wts[:, k:k + 1]
    g2 = mod_ref[:, :, 5 * d:6 * d]
    xn = x_ref[...] + g2 * acc.reshape(nb, ts, d)
    if final:
        xn = _rms(xn) * fg_ref[...]
    out_ref[...] = xn


def _combine(dest, y_sorted, wts_t, h2, x, mod, sw1, sw3, sw2, final_g, final):
    batch, seq, d = x.shape
    nb, ts = _seq_tile(batch, seq, 256)
    tc = nb * ts
    n_si = seq // ts
    tile = lambda: pl.BlockSpec((nb, ts, d), lambda bi, si: (bi, si, 0))
    full = lambda a: pl.BlockSpec(a.shape, lambda bi, si: (0,) * a.ndim)
    return pl.pallas_call(
        functools.partial(_combine_kernel, final=final),
        grid=(batch // nb, n_si),
        in_specs=[pl.BlockSpec((TOP_K, tc), lambda bi, si: (0, bi * n_si + si), memory_space=pltpu.SMEM),
                  pl.BlockSpec(memory_space=pl.ANY),
                  pl.BlockSpec((tc, TOP_K), lambda bi, si: (bi * n_si + si, 0)),
                  tile(), tile(), pl.BlockSpec((nb, 1, 6 * d), lambda bi, si: (bi, 0, 0)),
                  full(sw1), full(sw3), full(sw2), full(final_g)],
        out_specs=tile(),
        out_shape=jax.ShapeDtypeStruct((batch, seq, d), F32),
        scratch_shapes=[pltpu.VMEM((TOP_K, tc, d), F32), pltpu.SemaphoreType.DMA],
        compiler_params=_cparams("arbitrary", "arbitrary"),
        name="combine",
    )(dest, y_sorted, wts_t, h2, x, mod, sw1, sw3, sw2, final_g)


def _expert_block_rows(tokens):
    if tokens >= 16384:
        return 512
    if tokens >= 2048:
        return 256
    return 64


def _routing_tables(idx, pos, counts, rb, nblk):
    counts = counts.astype(I32)
    padded = (counts + rb - 1) // rb * rb
    pends = jnp.cumsum(padded)
    pstarts = pends - padded
    onehot = idx[..., None] == jnp.arange(N_EXPERTS, dtype=I32)
    dest = pos + jnp.sum(jnp.where(onehot, pstarts, 0), axis=-1)
    block_e = jnp.minimum(
        jnp.searchsorted(pends, jnp.arange(nblk, dtype=I32) * rb, side="right"), N_EXPERTS - 1).astype(I32)
    n_used = (pends[-1:] // rb).astype(I32)
    return dest, block_e, n_used


def _layer(x, mod, conv0, c0, n0, m0, p, final_g, final):
    batch, seq, d = x.shape
    tokens = batch * seq
    pc, pm, gcol, grow = _inproj(x, mod, p["norm1_g"], p["w_main"], p["wg_col"], p["wg_row"],
                                 p["b_col"], p["b_row"])
    grow = grow.reshape(batch, N_GATE, seq // CHUNK, CHUNK).transpose(0, 2, 1, 3)
    m0b = jnp.broadcast_to(jnp.pad(m0, ((0, 0), (0, 8 - ML_H)))[:, :, None], (batch, 8, LANES))
    n0b = jnp.pad(n0, ((0, 0), (0, 8 - ML_H), (0, 0)))
    x1, h2, logits_t, conv_new, c_new, n_new, m_new = _mixer(
        x, mod, pc, pm, gcol, grow, conv0, c0, n0b, m0b, p["conv_w"], p["conv_norm_g"],
        p["mlstm_norm_g"], p["w_out"], p["norm2_g"], p["router_wt"])
    idx, wts, pos, cnt = _route(logits_t, p["router_b"])
    rb = _expert_block_rows(tokens)
    nblk = tokens * TOP_K // rb + N_EXPERTS
    dest, block_e, n_used = _routing_tables(idx, pos, cnt[:, 0], rb, nblk)
    xs = _dispatch(dest, h2.reshape(tokens, d), nblk * rb)
    ys = _experts(block_e, n_used, xs, p["exp_w1"], p["exp_w3"], p["exp_w2"], rb)
    out = _combine(dest, ys, wts.T, h2, x1, mod, p["shared_w1"], p["shared_w3"], p["shared_w2"],
                   final_g, final)
    return out, conv_new, c_new, n_new[:, :ML_H, :], m_new[:, :ML_H, 0]


def _trunk(x, mods, conv0, c0, n0, m0, layers, final_g):
    depth = len(layers)
    convs, cs, ns, ms = [], [], [], []
    for l in range(depth):
        x, cv, c, n, m = _layer(x, mods[l], conv0[l], c0[l], n0[l], m0[l], layers[l], final_g,
                                final=(l == depth - 1))
        convs.append(cv); cs.append(c); ns.append(n); ms.append(m)
    return x, jnp.stack(convs), jnp.stack(cs), jnp.stack(ns), jnp.stack(ms)


def kernel(x_prompt, x_sample, c_prompt, c_sample, cache_conv, state_mlstm_C, state_mlstm_n, state_mlstm_m,
           norm1_g, ada_w, ada_b, w_in, b_igate, b_fgate, conv_w, conv_norm_g, mlstm_norm_g, w_out, norm2_g,
           router_w, router_b, exp_w1, exp_w3, exp_w2, shared_w1, shared_w3, shared_w2, final_g):
    depth = w_in.shape[0]
    bp = x_prompt.shape[0]
    d = x_prompt.shape[-1]
    mods = _ada(jnp.concatenate([c_prompt, c_sample], axis=0), ada_w, ada_b)
    mods_p = mods[:, :bp, None, :]
    mods_s = mods[:, bp:, None, :]

    w_gate = w_in[:, :, N_MAIN:]
    bias = jnp.concatenate([b_igate, b_fgate], axis=-1)
    layers = []
    for l in range(depth):
        layers.append(dict(
            norm1_g=norm1_g[l][None, :],
            w_main=w_in[l, :, :N_MAIN].astype(BF16),
            wg_col=jnp.pad(w_gate[l], ((0, 0), (0, LANES - N_GATE))),
            wg_row=w_gate[l].T,
            b_col=jnp.pad(bias[l], (0, LANES - N_GATE))[None, :],
            b_row=bias[l][:, None],
            conv_w=conv_w[l], conv_norm_g=conv_norm_g[l][None, :], mlstm_norm_g=mlstm_norm_g[l][None, :],
            w_out=w_out[l].astype(BF16), norm2_g=norm2_g[l][None, :],
            router_wt=router_w[l].T, router_b=router_b[l],
            exp_w1=exp_w1[l].astype(BF16), exp_w3=exp_w3[l].astype(BF16), exp_w2=exp_w2[l].astype(BF16),
            shared_w1=shared_w1[l].astype(BF16), shared_w3=shared_w3[l].astype(BF16),
            shared_w2=shared_w2[l].astype(BF16)))
    fg = final_g[None, :]

    zeros = lambda *s: jnp.zeros((depth, bp) + s, x_prompt.dtype)
    y_p, conv_p, c_p, n_p, m_p = _trunk(
        x_prompt, mods_p, zeros(CONV_K - 1, CONV_W), zeros(ML_H, ML_DK, ML_DK), zeros(ML_H, ML_DK),
        zeros(ML_H), layers, fg)
    y_s, conv_s, c_s, n_s, m_s = _trunk(
        x_sample, mods_s, cache_conv, state_mlstm_C, state_mlstm_n, state_mlstm_m, layers, fg)
    return (y_p, y_s, conv_p, c_p, n_p, m_p, conv_s, c_s, n_s, m_s)
```

```python
import functools

import jax
import jax.numpy as jnp
import numpy as np
from jax import lax
from jax.experimental import pallas as pl
from jax.experimental.pallas import tpu as pltpu

F32 = jnp.float32
BF16 = jnp.bfloat16
I32 = jnp.int32
U32 = jnp.uint32
HI_MASK = np.uint32(0xFFFF0000)
HIGHEST = lax.Precision.HIGHEST

D_MODEL = 1024
CONV_W = 512
CONV_K = 3
ML_W = 512
ML_H = 4
ML_DK = 128
CHUNK = 64
N_EXPERTS = 64
TOP_K = 8
N_GROUPS = 8
GROUP_SIZE = N_EXPERTS // N_GROUPS
TOPK_GROUPS = 4
D_EXPERT = 256
ROUTED_SCALE = 2.5
EPS = 1e-6
K_SCALE = ML_DK ** -0.5
N_MAIN = 3 * CONV_W + 4 * ML_W
N_GATE = 2 * ML_H
LANES = 128
ROW_TILE = 512
VMEM_LIMIT = 56 * 1024 * 1024

NT_DIMS = (((1,), (1,)), ((), ()))


def _cparams(*sem):
    return pltpu.CompilerParams(dimension_semantics=sem, vmem_limit_bytes=VMEM_LIMIT)


def _rms(x):
    return x * lax.rsqrt(jnp.mean(x * x, axis=-1, keepdims=True) + EPS)


def _log_sigmoid(x):
    return jnp.minimum(x, 0.0) - jnp.log1p(jnp.exp(-jnp.abs(x)))


def _pack_pairs(x):
    n = x.shape[-1] // 2
    lo = lax.bitcast_convert_type(x[..., :n].astype(BF16).astype(F32), U32)
    hi = lax.bitcast_convert_type(x[..., n:].astype(BF16).astype(F32), U32)
    return (lo >> 16) | (hi & HI_MASK)


def _unpack_pairs(w):
    lo = lax.bitcast_convert_type(w << 16, F32)
    hi = lax.bitcast_convert_type(w & HI_MASK, F32)
    return lo, hi


def _seq_tile(batch, seq, rows):
    if seq >= rows:
        return 1, rows
    return min(batch, rows // seq), seq


def _ada_kernel(c_ref, w_ref, b_ref, o_ref):
    c = c_ref[...]
    s = (c * jax.nn.sigmoid(c)).astype(BF16)
    o_ref[0] = jnp.dot(s, w_ref[0].astype(BF16), preferred_element_type=F32) + b_ref[0]


def _ada(c_all, ada_w, ada_b):
    depth, d, n = ada_w.shape
    nb = c_all.shape[0]
    tn = 1536
    return pl.pallas_call(
        _ada_kernel,
        grid=(depth, n // tn),
        in_specs=[
            pl.BlockSpec((nb, d), lambda l, j: (0, 0)),
            pl.BlockSpec((1, d, tn), lambda l, j: (l, 0, j)),
            pl.BlockSpec((1, 1, tn), lambda l, j: (l, 0, j)),
        ],
        out_specs=pl.BlockSpec((1, nb, tn), lambda l, j: (l, 0, j)),
        out_shape=jax.ShapeDtypeStruct((depth, nb, n), F32),
        compiler_params=_cparams("parallel", "parallel"),
        name="ada",
    )(c_all, ada_w, ada_b.reshape(depth, 1, n))


def _inproj_kernel(x_ref, mod_ref, g_ref, wm_ref, wgc_ref, wgr_ref, bc_ref, br_ref,
                   pc_ref, pm_ref, gc_ref, gr_ref):
    nb, ts, d = x_ref.shape
    m = nb * ts
    sh = mod_ref[:, :, 0:d]
    sc = mod_ref[:, :, d:2 * d]
    h = (_rms(x_ref[...]) * g_ref[...] * (1.0 + sc) + sh).reshape(m, d)
    hb = h.astype(BF16)
    cn = 512
    for j in range(N_MAIN // cn):
        pj = jnp.dot(hb, wm_ref[:, j * cn:(j + 1) * cn], preferred_element_type=F32).astype(BF16)
        if j < 3:
            pc_ref[:, :, j * cn:(j + 1) * cn] = pj.reshape(nb, ts, cn)
        else:
            pm_ref[:, :, (j - 3) * cn:(j - 2) * cn] = pj.reshape(nb, ts, cn)
    gcol = jnp.dot(h, wgc_ref[...], precision=HIGHEST, preferred_element_type=F32) + bc_ref[...]
    lane = lax.broadcasted_iota(I32, gcol.shape, 1)
    gc_ref[...] = jnp.where(lane >= ML_H, _log_sigmoid(gcol), gcol).reshape(nb, ts, LANES)
    grow = lax.dot_general(wgr_ref[...], h, NT_DIMS, precision=HIGHEST,
                           preferred_element_type=F32) + br_ref[...]
    row = lax.broadcasted_iota(I32, grow.shape, 0)
    grow = jnp.where(row >= ML_H, _log_sigmoid(grow), grow)
    for b in range(nb):
        gr_ref[b] = grow[:, b * ts:(b + 1) * ts]


def _inproj(x, mod, norm_g, w_main, wg_col, wg_row, b_col, b_row):
    batch, seq, d = x.shape
    nb, ts = _seq_tile(batch, seq, ROW_TILE)
    grid = (batch // nb, seq // ts)
    tile = lambda w: pl.BlockSpec((nb, ts, w), lambda bi, si: (bi, si, 0))
    full = lambda a: pl.BlockSpec(a.shape, lambda bi, si: (0,) * a.ndim)
    return pl.pallas_call(
        _inproj_kernel,
        grid=grid,
        in_specs=[tile(d), pl.BlockSpec((nb, 1, 6 * d), lambda bi, si: (bi, 0, 0)),
                  full(norm_g), full(w_main), full(wg_col), full(wg_row), full(b_col), full(b_row)],
        out_specs=[tile(3 * CONV_W), tile(4 * ML_W), tile(LANES),
                   pl.BlockSpec((nb, N_GATE, ts), lambda bi, si: (bi, 0, si))],
        out_shape=[jax.ShapeDtypeStruct((batch, seq, 3 * CONV_W), BF16),
                   jax.ShapeDtypeStruct((batch, seq, 4 * ML_W), BF16),
                   jax.ShapeDtypeStruct((batch, seq, LANES), F32),
                   jax.ShapeDtypeStruct((batch, N_GATE, seq), F32)],
        compiler_params=_cparams("parallel", "parallel"),
        name="inproj",
    )(x, mod, norm_g, w_main, wg_col, wg_row, b_col, b_row)


def _mixer_kernel(x_ref, mod_ref, pc_ref, pm_ref, gc_ref, gr_ref, conv0_ref, c0_ref, n0_ref, m0_ref,
                  convw_ref, cng_ref, mng_ref, wout_ref, n2g_ref, rwt_ref,
                  xo_ref, h2_ref, lt_ref, convo_ref, co_ref, no_ref, mo_ref,
                  ubuf, c_s, n_s, m_s, hm_s):
    nb, ts, d = x_ref.shape
    nch = ts // CHUNK
    m = nb * ts
    si = pl.program_id(1)
    hdr = 8

    @pl.when(si == 0)
    def _load_state():
        ubuf[:, hdr - 2:hdr, :] = conv0_ref[...]
        c_s[...] = c0_ref[...]
        n_s[...] = n0_ref[...]
        m_s[...] = m0_ref[...]

    cb = pc_ref[:, :, 0:CONV_W].astype(F32)
    cc = pc_ref[:, :, CONV_W:2 * CONV_W].astype(F32)
    ch = pc_ref[:, :, 2 * CONV_W:3 * CONV_W].astype(F32)
    ubuf[:, hdr:hdr + ts, :] = cc * ch
    w = convw_ref[...]
    y = (w[0:1] * ubuf[:, hdr - 2:hdr - 2 + ts, :] + w[1:2] * ubuf[:, hdr - 1:hdr - 1 + ts, :]
         + w[2:3] * ubuf[:, hdr:hdr + ts, :])
    co = (_rms(cb * y) * cng_ref[...]).reshape(m, CONV_W).astype(BF16)
    last2 = ubuf[:, hdr + ts - 2:hdr + ts, :]
    ubuf[:, hdr - 2:hdr, :] = last2

    r_i = lax.broadcasted_iota(I32, (CHUNK, CHUNK), 0)
    c_i = lax.broadcasted_iota(I32, (CHUNK, CHUNK), 1)
    causal = r_i >= c_i
    ltri = causal.astype(F32)
    utri = (r_i <= c_i).astype(F32)

    def unit(u, carry):
        if nch == 1:
            b, c = u, 0
        elif nb == 1:
            b, c = 0, u
        else:
            b, c = u // nch, u % nch
        t0 = pl.multiple_of(c * CHUNK, CHUNK)
        rows = pl.ds(t0, CHUNK)
        gcol = gc_ref[b, rows, :]
        grow = gr_ref[b, c]
        f_col = jnp.dot(ltri, gcol, precision=HIGHEST, preferred_element_type=F32)
        f_row = jnp.dot(grow, utri, precision=HIGHEST, preferred_element_type=F32)
        for h in range(ML_H):
            q = pm_ref[b, rows, h * ML_DK:(h + 1) * ML_DK]
            k = pm_ref[b, rows, ML_W + h * ML_DK:ML_W + (h + 1) * ML_DK]
            v = pm_ref[b, rows, 2 * ML_W + h * ML_DK:2 * ML_W + (h + 1) * ML_DK]
            og = pm_ref[b, rows, 3 * ML_W + h * ML_DK:3 * ML_W + (h + 1) * ML_DK]
            fc = f_col[:, ML_H + h:ML_H + h + 1]
            fr = f_row[ML_H + h:ML_H + h + 1, :]
            igc = gcol[:, h:h + 1]
            igr = grow[h:h + 1, :]
            dmat = jnp.where(causal, fc - fr + igr, -jnp.inf)
            m_prev = m_s[b, h:h + 1, 0:1]
            prior = fc + m_prev
            m_t = jnp.maximum(prior, jnp.max(dmat, axis=1, keepdims=True))
            w_prior = jnp.exp(prior - m_t)
            p = jnp.exp(dmat - m_t)
            qk = lax.dot_general(q, k, NT_DIMS, preferred_element_type=F32)
            s = qk * (p * K_SCALE)
            c_old = c_s[b, h]
            n_old = n_s[b, h:h + 1, :]
            num = (jnp.dot(s.astype(BF16), v, preferred_element_type=F32)
                   + w_prior * jnp.dot(q, c_old.astype(BF16), preferred_element_type=F32))
            den = (jnp.sum(s, axis=1, keepdims=True)
                   + w_prior * jnp.sum(q.astype(F32) * n_old, axis=1, keepdims=True))
            hh = num / jnp.maximum(jnp.abs(den), jnp.exp(-m_t))
            m_new = m_t[CHUNK - 1:CHUNK, :]
            f_last = fc[CHUNK - 1:CHUNK, :]
            a = jnp.exp(f_last + m_prev - m_new)
            ws = jnp.exp(f_last - fc + igc - m_new) * K_SCALE
            kw = k.astype(F32) * ws
            c_s[b, h] = a * c_old + jnp.dot(kw.T.astype(BF16), v, preferred_element_type=F32)
            n_s[b, h:h + 1, :] = a * n_old + jnp.sum(kw, axis=0, keepdims=True)
            m_s[b, h:h + 1, :] = jnp.broadcast_to(m_new, (1, LANES))
            hn = _rms(hh) * mng_ref[:, h * ML_DK:(h + 1) * ML_DK]
            hn = hn * jax.nn.sigmoid(og.astype(F32))
            hm_s[pl.ds(pl.multiple_of(u * CHUNK, CHUNK), CHUNK), h * ML_DK:(h + 1) * ML_DK] = hn
        return carry

    lax.fori_loop(0, nb * nch, unit, 0)

    mix = (jnp.dot(co, wout_ref[0:CONV_W, :], preferred_element_type=F32)
           + jnp.dot(hm_s[...].astype(BF16), wout_ref[CONV_W:CONV_W + ML_W, :], preferred_element_type=F32))
    g1 = mod_ref[:, :, 2 * d:3 * d]
    sh2 = mod_ref[:, :, 3 * d:4 * d]
    sc2 = mod_ref[:, :, 4 * d:5 * d]
    xn = x_ref[...] + g1 * mix.reshape(nb, ts, d)
    xo_ref[...] = xn
    h2 = _rms(xn) * n2g_ref[...] * (1.0 + sc2) + sh2
    h2_ref[...] = _pack_pairs(h2)
    lt_ref[...] = lax.dot_general(rwt_ref[...], h2.reshape(m, d), NT_DIMS, precision=HIGHEST,
                                  preferred_element_type=F32)

    @pl.when(si == pl.num_programs(1) - 1)
    def _store_state():
        convo_ref[...] = last2
        co_ref[...] = c_s[...]
        no_ref[...] = n_s[...]
        mo_ref[...] = m_s[...]


def _mixer(x, mod, pc, pm, gcol, grow, conv0, c0, n0, m0, conv_w, cn_g, mn_g, w_out, n2_g, rw_t):
    batch, seq, d = x.shape
    nb, ts = _seq_tile(batch, seq, ROW_TILE)
    nch = ts // CHUNK
    n_si = seq // ts
    grid = (batch // nb, n_si)
    tile = lambda w: pl.BlockSpec((nb, ts, w), lambda bi, si: (bi, si, 0))
    full = lambda a: pl.BlockSpec(a.shape, lambda bi, si: (0,) * a.ndim)
    per_b = lambda *tail: pl.BlockSpec((nb,) + tail, lambda bi, si: (bi,) + (0,) * len(tail))
    tokens = batch * seq
    return pl.pallas_call(
        _mixer_kernel,
        grid=grid,
        in_specs=[tile(d), per_b(1, 6 * d), tile(3 * CONV_W), tile(4 * ML_W), tile(LANES),
                  pl.BlockSpec((nb, nch, N_GATE, CHUNK), lambda bi, si: (bi, si, 0, 0)),
                  per_b(CONV_K - 1, CONV_W), per_b(ML_H, ML_DK, ML_DK), per_b(8, LANES), per_b(8, LANES),
                  full(conv_w), full(cn_g), full(mn_g), full(w_out), full(n2_g), full(rw_t)],
        out_specs=[tile(d), tile(d // 2),
                   pl.BlockSpec((N_EXPERTS, nb * ts), lambda bi, si: (0, bi * n_si + si)),
                   per_b(CONV_K - 1, CONV_W), per_b(ML_H, ML_DK, ML_DK), per_b(8, LANES), per_b(8, LANES)],
        out_shape=[jax.ShapeDtypeStruct((batch, seq, d), F32),
                   jax.ShapeDtypeStruct((batch, seq, d // 2), U32),
                   jax.ShapeDtypeStruct((N_EXPERTS, tokens), F32),
                   jax.ShapeDtypeStruct((batch, CONV_K - 1, CONV_W), F32),
                   jax.ShapeDtypeStruct((batch, ML_H, ML_DK, ML_DK), F32),
                   jax.ShapeDtypeStruct((batch, 8, LANES), F32),
                   jax.ShapeDtypeStruct((batch, 8, LANES), F32)],
        scratch_shapes=[pltpu.VMEM((nb, ts + 8, CONV_W), F32),
                        pltpu.VMEM((nb, ML_H, ML_DK, ML_DK), F32),
                        pltpu.VMEM((nb, 8, LANES), F32),
                        pltpu.VMEM((nb, 8, LANES), F32),
                        pltpu.VMEM((nb * ts, ML_W), F32)],
        compiler_params=_cparams("parallel", "arbitrary"),
        name="mixer",
    )(x, mod, pc, pm, gcol, grow, conv0, c0, n0, m0, conv_w, cn_g, mn_g, w_out, n2_g, rw_t)


def _route_kernel(lt_ref, rb_ref, tri_ref, idx_ref, wts_ref, pos_ref, cnt_ref, cnt_s):
    tt = lt_ref.shape[1]

    @pl.when(pl.program_id(0) == 0)
    def _zero():
        cnt_s[...] = jnp.zeros_like(cnt_s)

    neg = -jnp.inf
    scores = jax.nn.sigmoid(lt_ref[...])
    sel3 = (scores + rb_ref[...]).reshape(N_GROUPS, GROUP_SIZE, tt)
    sc3 = scores.reshape(N_GROUPS, GROUP_SIZE, tt)
    shape3 = (N_GROUPS, GROUP_SIZE, tt)
    j3 = lax.broadcasted_iota(I32, shape3, 1).astype(F32)
    g3 = lax.broadcasted_iota(I32, shape3, 0).astype(F32)
    e3 = g3 * GROUP_SIZE + j3
    m1 = jnp.max(sel3, axis=1, keepdims=True)
    first = jnp.min(jnp.where(sel3 == m1, j3, float(GROUP_SIZE)), axis=1, keepdims=True)
    m2 = jnp.max(jnp.where(j3 == first, neg, sel3), axis=1, keepdims=True)
    grp = m1 + m2
    gi = lax.broadcasted_iota(I32, grp.shape, 0).astype(F32)
    gmask = jnp.zeros(grp.shape, F32)
    for _ in range(TOPK_GROUPS):
        gmax = jnp.max(grp, axis=0, keepdims=True)
        gidx = jnp.min(jnp.where(grp == gmax, gi, float(N_GROUPS)), axis=0, keepdims=True)
        hit = gi == gidx
        gmask = jnp.where(hit, 1.0, gmask)
        grp = jnp.where(hit, neg, grp)
    cand = jnp.where(gmask > 0.0, sel3, neg)

    def red(fn, x):
        return fn(fn(x, axis=1, keepdims=True), axis=0, keepdims=True)

    selm = jnp.zeros(shape3, F32)
    idxs, ws = [], []
    for _ in range(TOP_K):
        cmax = red(jnp.max, cand)
        eidx = red(jnp.min, jnp.where(cand == cmax, e3, float(N_EXPERTS)))
        hit = e3 == eidx
        idxs.append(eidx)
        ws.append(red(jnp.sum, jnp.where(hit, sc3, 0.0)))
        selm = jnp.where(hit, 1.0, selm)
        cand = jnp.where(hit, neg, cand)
    wsum = ws[0]
    for k in range(1, TOP_K):
        wsum = wsum + ws[k]
    cum = jnp.dot(selm.reshape(N_EXPERTS, tt).astype(BF16), tri_ref[...],
                  preferred_element_type=F32) + cnt_s[:, 0:1]
    cum3 = cum.reshape(shape3)
    for k in range(TOP_K):
        hit = e3 == idxs[k]
        idx_ref[k:k + 1, :] = idxs[k].reshape(1, tt).astype(I32)
        wts_ref[k:k + 1, :] = (ws[k] / wsum * ROUTED_SCALE).reshape(1, tt)
        pos_ref[k:k + 1, :] = red(jnp.sum, jnp.where(hit, cum3, 0.0)).reshape(1, tt).astype(I32)
    cnt_s[...] = cnt_s[...] + jnp.sum(selm.reshape(N_EXPERTS, tt), axis=1, keepdims=True)
    cnt_ref[...] = cnt_s[...]


def _route(logits_t, router_b):
    tokens = logits_t.shape[1]
    tt = min(ROW_TILE, tokens)
    r_i = lax.broadcasted_iota(I32, (tt, tt), 0)
    c_i = lax.broadcasted_iota(I32, (tt, tt), 1)
    tri = (r_i < c_i).astype(BF16)
    row = lambda: pl.BlockSpec((TOP_K, tt), lambda i: (0, i))
    return pl.pallas_call(
        _route_kernel,
        grid=(tokens // tt,),
        in_specs=[pl.BlockSpec((N_EXPERTS, tt), lambda i: (0, i)),
                  pl.BlockSpec((N_EXPERTS, 1), lambda i: (0, 0)),
                  pl.BlockSpec((tt, tt), lambda i: (0, 0))],
        out_specs=[row(), row(), row(), pl.BlockSpec((N_EXPERTS, LANES), lambda i: (0, 0))],
        out_shape=[jax.ShapeDtypeStruct((TOP_K, tokens), I32),
                   jax.ShapeDtypeStruct((TOP_K, tokens), F32),
                   jax.ShapeDtypeStruct((TOP_K, tokens), I32),
                   jax.ShapeDtypeStruct((N_EXPERTS, LANES), F32)],
        scratch_shapes=[pltpu.VMEM((N_EXPERTS, LANES), F32)],
        compiler_params=_cparams("arbitrary"),
        name="route",
    )(logits_t, router_b.reshape(N_EXPERTS, 1), tri)


ISSUE_UNROLL = 8


def _dispatch_kernel(dest_ref, h_ref, xs_in, xs_hbm, sem):
    del xs_in
    td = dest_ref.shape[1]

    def issue(t, carry):
        for k in range(TOP_K):
            pltpu.make_async_copy(h_ref.at[pl.ds(t, 1)],
                                  xs_hbm.at[pl.ds(dest_ref[k, t], 1)], sem).start(priority=k % 2)
        return carry

    lax.fori_loop(0, td, issue, 0, unroll=ISSUE_UNROLL)
    for _ in range(TOP_K):
        pltpu.make_async_copy(h_ref, xs_hbm.at[pl.ds(0, td)], sem).wait()


def _dispatch(dest, h2p_flat, rows_total):
    tokens, dw = h2p_flat.shape
    td = min(ROW_TILE, tokens)
    zeros = jnp.zeros((rows_total, dw), h2p_flat.dtype)
    return pl.pallas_call(
        _dispatch_kernel,
        grid=(tokens // td,),
        in_specs=[pl.BlockSpec((TOP_K, td), lambda i: (0, i), memory_space=pltpu.SMEM),
                  pl.BlockSpec((td, dw), lambda i: (i, 0)),
                  pl.BlockSpec(memory_space=pl.ANY)],
        out_specs=pl.BlockSpec(memory_space=pl.ANY),
        out_shape=jax.ShapeDtypeStruct((rows_total, dw), h2p_flat.dtype),
        scratch_shapes=[pltpu.SemaphoreType.DMA],
        input_output_aliases={2: 0},
        compiler_params=_cparams("arbitrary"),
        name="dispatch",
    )(dest, h2p_flat, zeros)


def _swiglu_packed(xw, w1, w3, w2):
    half = xw.shape[-1]
    lo, hi = _unpack_pairs(xw)
    lo = lo.astype(BF16)
    hi = hi.astype(BF16)
    a = (jnp.dot(lo, w1[:half], preferred_element_type=F32)
         + jnp.dot(hi, w1[half:], preferred_element_type=F32))
    b = (jnp.dot(lo, w3[:half], preferred_element_type=F32)
         + jnp.dot(hi, w3[half:], preferred_element_type=F32))
    hmid = (a * jax.nn.sigmoid(a) * b).astype(BF16)
    return jnp.dot(hmid, w2, preferred_element_type=F32)


def _expert_kernel(be_ref, nu_ref, x_ref, w1_ref, w3_ref, w2_ref, y_ref):
    del be_ref
    j = pl.program_id(0)

    @pl.when(j < nu_ref[0])
    def _compute():
        y_ref[...] = _pack_pairs(_swiglu_packed(x_ref[...], w1_ref[0], w3_ref[0], w2_ref[0]))

    @pl.when(j >= nu_ref[0])
    def _unused():
        y_ref[...] = jnp.zeros_like(y_ref)


def _experts(block_e, n_used, xs, w1, w3, w2, rb):
    rows, dw = xs.shape
    nblk = rows // rb
    d, f = w1.shape[1:]
    return pl.pallas_call(
        _expert_kernel,
        grid_spec=pltpu.PrefetchScalarGridSpec(
            num_scalar_prefetch=2,
            grid=(nblk,),
            in_specs=[pl.BlockSpec((rb, dw), lambda j, be, nu: (jnp.minimum(j, nu[0] - 1), 0)),
                      pl.BlockSpec((1, d, f), lambda j, be, nu: (be[j], 0, 0)),
                      pl.BlockSpec((1, d, f), lambda j, be, nu: (be[j], 0, 0)),
                      pl.BlockSpec((1, f, d), lambda j, be, nu: (be[j], 0, 0))],
            out_specs=pl.BlockSpec((rb, dw), lambda j, be, nu: (j, 0)),
        ),
        out_shape=jax.ShapeDtypeStruct((rows, dw), U32),
        compiler_params=_cparams("arbitrary"),
        name="experts",
    )(block_e, n_used, xs, w1, w3, w2)


def _combine_kernel(dest_ref, y_hbm, wts_ref, h2_ref, x_ref, mod_ref, sw1_ref, sw3_ref, sw2_ref, fg_ref,
                    out_ref, buf, sem, *, final):
    nb, ts, d = x_ref.shape
    tc = nb * ts
    dw = d // 2

    def issue(t, carry):
        for k in range(TOP_K):
            pltpu.make_async_copy(y_hbm.at[pl.ds(dest_ref[k, t], 1)],
                                  buf.at[k, pl.ds(t, 1)], sem).start(priority=k % 2)
        return carry

    lax.fori_loop(0, tc, issue, 0, unroll=ISSUE_UNROLL)

    shared = _swiglu_packed(h2_ref[...].reshape(tc, dw), sw1_ref[...], sw3_ref[...], sw2_ref[...])

    for k in range(TOP_K):
        pltpu.make_async_copy(y_hbm.at[pl.ds(0, tc)], buf.at[k], sem).wait()
    wts = wts_ref[...]
    acc_lo = shared[:, :dw]
    acc_hi = shared[:, dw:]
    for k in range(TOP_K):
        lo, hi = _unpack_pairs(buf[k])
        wk = wts[:, k:k + 1]
        acc_lo = acc_lo + lo * wk
        acc_hi = acc_hi + hi * wk
    acc = jnp.concatenate([acc_lo, acc_hi], axis=-1)
    g2 = mod_ref[:, :, 5 * d:6 * d]
    xn = x_ref[...] + g2 * acc.reshape(nb, ts, d)
    if final:
        xn = _rms(xn) * fg_ref[...]
    out_ref[...] = xn


def _combine(dest, y_sorted, wts_t, h2p, x, mod, sw1, sw3, sw2, final_g, final):
    batch, seq, d = x.shape
    dw = d // 2
    nb, ts = _seq_tile(batch, seq, 256)
    tc = nb * ts
    n_si = seq // ts
    tile = lambda w: pl.BlockSpec((nb, ts, w), lambda bi, si: (bi, si, 0))
    full = lambda a: pl.BlockSpec(a.shape, lambda bi, si: (0,) * a.ndim)
    return pl.pallas_call(
        functools.partial(_combine_kernel, final=final),
        grid=(batch // nb, n_si),
        in_specs=[pl.BlockSpec((TOP_K, tc), lambda bi, si: (0, bi * n_si + si), memory_space=pltpu.SMEM),
                  pl.BlockSpec(memory_space=pl.ANY),
                  pl.BlockSpec((tc, TOP_K), lambda bi, si: (bi * n_si + si, 0)),
                  tile(dw), tile(d), pl.BlockSpec((nb, 1, 6 * d), lambda bi, si: (bi, 0, 0)),
                  full(sw1), full(sw3), full(sw2), full(final_g)],
        out_specs=tile(d),
        out_shape=jax.ShapeDtypeStruct((batch, seq, d), F32),
        scratch_shapes=[pltpu.VMEM((TOP_K, tc, dw), U32), pltpu.SemaphoreType.DMA],
        compiler_params=_cparams("arbitrary", "arbitrary"),
        name="combine",
    )(dest, y_sorted, wts_t, h2p, x, mod, sw1, sw3, sw2, final_g)


def _expert_block_rows(tokens):
    if tokens >= 16384:
        return 512
    if tokens >= 2048:
        return 256
    return 64


def _routing_tables(idx, pos, counts, rb, nblk):
    counts = counts.astype(I32)
    padded = (counts + rb - 1) // rb * rb
    pends = jnp.cumsum(padded)
    pstarts = pends - padded
    onehot = idx[..., None] == jnp.arange(N_EXPERTS, dtype=I32)
    dest = pos + jnp.sum(jnp.where(onehot, pstarts, 0), axis=-1)
    block_start = jnp.arange(nblk, dtype=I32) * rb
    block_e = jnp.minimum(jnp.sum(pends[None, :] <= block_start[:, None], axis=1), N_EXPERTS - 1).astype(I32)
    n_used = (pends[-1:] // rb).astype(I32)
    return dest, block_e, n_used


def _layer(x, mod, conv0, c0, n0, m0, p, final_g, final):
    batch, seq, d = x.shape
    tokens = batch * seq
    pc, pm, gcol, grow = _inproj(x, mod, p["norm1_g"], p["w_main"], p["wg_col"], p["wg_row"],
                                 p["b_col"], p["b_row"])
    grow = grow.reshape(batch, N_GATE, seq // CHUNK, CHUNK).transpose(0, 2, 1, 3)
    m0b = jnp.broadcast_to(jnp.pad(m0, ((0, 0), (0, 8 - ML_H)))[:, :, None], (batch, 8, LANES))
    n0b = jnp.pad(n0, ((0, 0), (0, 8 - ML_H), (0, 0)))
    x1, h2, logits_t, conv_new, c_new, n_new, m_new = _mixer(
        x, mod, pc, pm, gcol, grow, conv0, c0, n0b, m0b, p["conv_w"], p["conv_norm_g"],
        p["mlstm_norm_g"], p["w_out"], p["norm2_g"], p["router_wt"])
    idx, wts, pos, cnt = _route(logits_t, p["router_b"])
    rb = _expert_block_rows(tokens)
    nblk = tokens * TOP_K // rb + N_EXPERTS
    dest, block_e, n_used = _routing_tables(idx, pos, cnt[:, 0], rb, nblk)
    xs = _dispatch(dest, h2.reshape(tokens, d // 2), nblk * rb)
    ys = _experts(block_e, n_used, xs, p["exp_w1"], p["exp_w3"], p["exp_w2"], rb)
    out = _combine(dest, ys, wts.T, h2, x1, mod, p["shared_w1"], p["shared_w3"], p["shared_w2"],
                   final_g, final)
    return out, conv_new, c_new, n_new[:, :ML_H, :], m_new[:, :ML_H, 0]


def _trunk(x, mods, conv0, c0, n0, m0, layers, final_g):
    depth = len(layers)
    convs, cs, ns, ms = [], [], [], []
    for l in range(depth):
        x, cv, c, n, m = _layer(x, mods[l], conv0[l], c0[l], n0[l], m0[l], layers[l], final_g,
                                final=(l == depth - 1))
        convs.append(cv); cs.append(c); ns.append(n); ms.append(m)
    return x, jnp.stack(convs), jnp.stack(cs), jnp.stack(ns), jnp.stack(ms)


def kernel(x_prompt, x_sample, c_prompt, c_sample, cache_conv, state_mlstm_C, state_mlstm_n, state_mlstm_m,
           norm1_g, ada_w, ada_b, w_in, b_igate, b_fgate, conv_w, conv_norm_g, mlstm_norm_g, w_out, norm2_g,
           router_w, router_b, exp_w1, exp_w3, exp_w2, shared_w1, shared_w3, shared_w2, final_g):
    depth = w_in.shape[0]
    bp = x_prompt.shape[0]
    d = x_prompt.shape[-1]
    mods = _ada(jnp.concatenate([c_prompt, c_sample], axis=0), ada_w, ada_b)
    mods_p = mods[:, :bp, None, :]
    mods_s = mods[:, bp:, None, :]

    w_gate = w_in[:, :, N_MAIN:]
    bias = jnp.concatenate([b_igate, b_fgate], axis=-1)
    layers = []
    for l in range(depth):
        layers.append(dict(
            norm1_g=norm1_g[l][None, :],
            w_main=w_in[l, :, :N_MAIN].astype(BF16),
            wg_col=jnp.pad(w_gate[l], ((0, 0), (0, LANES - N_GATE))),
            wg_row=w_gate[l].T,
            b_col=jnp.pad(bias[l], (0, LANES - N_GATE))[None, :],
            b_row=bias[l][:, None],
            conv_w=conv_w[l], conv_norm_g=conv_norm_g[l][None, :], mlstm_norm_g=mlstm_norm_g[l][None, :],
            w_out=w_out[l].astype(BF16), norm2_g=norm2_g[l][None, :],
            router_wt=router_w[l].T, router_b=router_b[l],
            exp_w1=exp_w1[l].astype(BF16), exp_w3=exp_w3[l].astype(BF16), exp_w2=exp_w2[l].astype(BF16),
            shared_w1=shared_w1[l].astype(BF16), shared_w3=shared_w3[l].astype(BF16),
            shared_w2=shared_w2[l].astype(BF16)))
    fg = final_g[None, :]

    zeros = lambda *s: jnp.zeros((depth, bp) + s, x_prompt.dtype)
    y_p, conv_p, c_p, n_p, m_p = _trunk(
        x_prompt, mods_p, zeros(CONV_K - 1, CONV_W), zeros(ML_H, ML_DK, ML_DK), zeros(ML_H, ML_DK),
        zeros(ML_H), layers, fg)
    y_s, conv_s, c_s, n_s, m_s = _trunk(
        x_sample, mods_s, cache_conv, state_mlstm_C, state_mlstm_n, state_mlstm_m, layers, fg)
    return (y_p, y_s, conv_p, c_p, n_p, m_p, conv_s, c_s, n_s, m_s)
```

```python
import functools

import jax
import jax.numpy as jnp
import numpy as np
from jax import lax
from jax.experimental import pallas as pl
from jax.experimental.pallas import tpu as pltpu

F32 = jnp.float32
BF16 = jnp.bfloat16
I32 = jnp.int32
U32 = jnp.uint32
HI_MASK = np.uint32(0xFFFF0000)
HIGHEST = lax.Precision.HIGHEST

D_MODEL = 1024
CONV_W = 512
CONV_K = 3
ML_W = 512
ML_H = 4
ML_DK = 128
CHUNK = 64
N_EXPERTS = 64
TOP_K = 8
N_GROUPS = 8
GROUP_SIZE = N_EXPERTS // N_GROUPS
TOPK_GROUPS = 4
D_EXPERT = 256
ROUTED_SCALE = 2.5
EPS = 1e-6
K_SCALE = ML_DK ** -0.5
N_MAIN = 3 * CONV_W + 4 * ML_W
N_GATE = 2 * ML_H
LANES = 128
ROW_TILE = 512
VMEM_LIMIT = 56 * 1024 * 1024

NT_DIMS = (((1,), (1,)), ((), ()))


def _cparams(*sem):
    return pltpu.CompilerParams(dimension_semantics=sem, vmem_limit_bytes=VMEM_LIMIT)


def _rms(x):
    return x * lax.rsqrt(jnp.mean(x * x, axis=-1, keepdims=True) + EPS)


def _log_sigmoid(x):
    return jnp.minimum(x, 0.0) - jnp.log1p(jnp.exp(-jnp.abs(x)))


def _pack_pairs(x):
    n = x.shape[-1] // 2
    lo = lax.bitcast_convert_type(x[..., :n].astype(BF16).astype(F32), U32)
    hi = lax.bitcast_convert_type(x[..., n:].astype(BF16).astype(F32), U32)
    return (lo >> 16) | (hi & HI_MASK)


def _unpack_pairs(w):
    lo = lax.bitcast_convert_type(w << 16, F32)
    hi = lax.bitcast_convert_type(w & HI_MASK, F32)
    return lo, hi


def _seq_tile(batch, seq, rows):
    if seq >= rows:
        return 1, rows
    return min(batch, rows // seq), seq


def _ada_kernel(c_ref, w_ref, b_ref, o_ref):
    c = c_ref[...]
    s = (c * jax.nn.sigmoid(c)).astype(BF16)
    o_ref[0] = jnp.dot(s, w_ref[0].astype(BF16), preferred_element_type=F32) + b_ref[0]


def _ada(c_all, ada_w, ada_b):
    depth, d, n = ada_w.shape
    nb = c_all.shape[0]
    tn = 1536
    return pl.pallas_call(
        _ada_kernel,
        grid=(depth, n // tn),
        in_specs=[
            pl.BlockSpec((nb, d), lambda l, j: (0, 0)),
            pl.BlockSpec((1, d, tn), lambda l, j: (l, 0, j)),
            pl.BlockSpec((1, 1, tn), lambda l, j: (l, 0, j)),
        ],
        out_specs=pl.BlockSpec((1, nb, tn), lambda l, j: (l, 0, j)),
        out_shape=jax.ShapeDtypeStruct((depth, nb, n), F32),
        compiler_params=_cparams("parallel", "parallel"),
        name="ada",
    )(c_all, ada_w, ada_b.reshape(depth, 1, n))


def _inproj_kernel(x_ref, mod_ref, g_ref, wm_ref, wgc_ref, wgr_ref, bc_ref, br_ref,
                   pc_ref, pm_ref, gc_ref, gr_ref):
    nb, ts, d = x_ref.shape
    m = nb * ts
    sh = mod_ref[:, :, 0:d]
    sc = mod_ref[:, :, d:2 * d]
    h = (_rms(x_ref[...]) * g_ref[...] * (1.0 + sc) + sh).reshape(m, d)
    hb = h.astype(BF16)
    cn = 512
    for j in range(N_MAIN // cn):
        pj = jnp.dot(hb, wm_ref[:, j * cn:(j + 1) * cn], preferred_element_type=F32).astype(BF16)
        if j < 3:
            pc_ref[:, :, j * cn:(j + 1) * cn] = pj.reshape(nb, ts, cn)
        else:
            pm_ref[:, :, (j - 3) * cn:(j - 2) * cn] = pj.reshape(nb, ts, cn)
    gcol = jnp.dot(h, wgc_ref[...], precision=HIGHEST, preferred_element_type=F32) + bc_ref[...]
    lane = lax.broadcasted_iota(I32, gcol.shape, 1)
    gc_ref[...] = jnp.where(lane >= ML_H, _log_sigmoid(gcol), gcol).reshape(nb, ts, LANES)
    grow = lax.dot_general(wgr_ref[...], h, NT_DIMS, precision=HIGHEST,
                           preferred_element_type=F32) + br_ref[...]
    row = lax.broadcasted_iota(I32, grow.shape, 0)
    grow = jnp.where(row >= ML_H, _log_sigmoid(grow), grow)
    for b in range(nb):
        gr_ref[b] = grow[:, b * ts:(b + 1) * ts]


def _inproj(x, mod, norm_g, w_main, wg_col, wg_row, b_col, b_row):
    batch, seq, d = x.shape
    nb, ts = _seq_tile(batch, seq, ROW_TILE)
    grid = (batch // nb, seq // ts)
    tile = lambda w: pl.BlockSpec((nb, ts, w), lambda bi, si: (bi, si, 0))
    full = lambda a: pl.BlockSpec(a.shape, lambda bi, si: (0,) * a.ndim)
    return pl.pallas_call(
        _inproj_kernel,
        grid=grid,
        in_specs=[tile(d), pl.BlockSpec((nb, 1, 6 * d), lambda bi, si: (bi, 0, 0)),
                  full(norm_g), full(w_main), full(wg_col), full(wg_row), full(b_col), full(b_row)],
        out_specs=[tile(3 * CONV_W), tile(4 * ML_W), tile(LANES),
                   pl.BlockSpec((nb, N_GATE, ts), lambda bi, si: (bi, 0, si))],
        out_shape=[jax.ShapeDtypeStruct((batch, seq, 3 * CONV_W), BF16),
                   jax.ShapeDtypeStruct((batch, seq, 4 * ML_W), BF16),
                   jax.ShapeDtypeStruct((batch, seq, LANES), F32),
                   jax.ShapeDtypeStruct((batch, N_GATE, seq), F32)],
        compiler_params=_cparams("parallel", "parallel"),
        name="inproj",
    )(x, mod, norm_g, w_main, wg_col, wg_row, b_col, b_row)


def _mixer_kernel(x_ref, mod_ref, pc_ref, pm_ref, gc_ref, gr_ref, conv0_ref, c0_ref, n0_ref, m0_ref,
                  convw_ref, cng_ref, mng_ref, wout_ref, n2g_ref, rwt_ref,
                  xo_ref, h2_ref, lt_ref, convo_ref, co_ref, no_ref, mo_ref,
                  ubuf, c_s, n_s, m_s, hm_s):
    nb, ts, d = x_ref.shape
    nch = ts // CHUNK
    m = nb * ts
    si = pl.program_id(1)
    hdr = 8

    @pl.when(si == 0)
    def _load_state():
        ubuf[:, hdr - 2:hdr, :] = conv0_ref[...]
        c_s[...] = c0_ref[...]
        n_s[...] = n0_ref[...]
        m_s[...] = m0_ref[...]

    cb = pc_ref[:, :, 0:CONV_W].astype(F32)
    cc = pc_ref[:, :, CONV_W:2 * CONV_W].astype(F32)
    ch = pc_ref[:, :, 2 * CONV_W:3 * CONV_W].astype(F32)
    ubuf[:, hdr:hdr + ts, :] = cc * ch
    w = convw_ref[...]
    y = (w[0:1] * ubuf[:, hdr - 2:hdr - 2 + ts, :] + w[1:2] * ubuf[:, hdr - 1:hdr - 1 + ts, :]
         + w[2:3] * ubuf[:, hdr:hdr + ts, :])
    co = (_rms(cb * y) * cng_ref[...]).reshape(m, CONV_W).astype(BF16)
    last2 = ubuf[:, hdr + ts - 2:hdr + ts, :]
    ubuf[:, hdr - 2:hdr, :] = last2

    r_i = lax.broadcasted_iota(I32, (CHUNK, CHUNK), 0)
    c_i = lax.broadcasted_iota(I32, (CHUNK, CHUNK), 1)
    causal = r_i >= c_i
    ltri = causal.astype(F32)
    utri = (r_i <= c_i).astype(F32)

    def unit(u, carry):
        if nch == 1:
            b, c = u, 0
        elif nb == 1:
            b, c = 0, u
        else:
            b, c = u // nch, u % nch
        t0 = pl.multiple_of(c * CHUNK, CHUNK)
        rows = pl.ds(t0, CHUNK)
        gcol = gc_ref[b, rows, :]
        grow = gr_ref[b, c]
        f_col = jnp.dot(ltri, gcol, precision=HIGHEST, preferred_element_type=F32)
        f_row = jnp.dot(grow, utri, precision=HIGHEST, preferred_element_type=F32)
        for h in range(ML_H):
            q = pm_ref[b, rows, h * ML_DK:(h + 1) * ML_DK]
            k = pm_ref[b, rows, ML_W + h * ML_DK:ML_W + (h + 1) * ML_DK]
            v = pm_ref[b, rows, 2 * ML_W + h * ML_DK:2 * ML_W + (h + 1) * ML_DK]
            og = pm_ref[b, rows, 3 * ML_W + h * ML_DK:3 * ML_W + (h + 1) * ML_DK]
            fc = f_col[:, ML_H + h:ML_H + h + 1]
            fr = f_row[ML_H + h:ML_H + h + 1, :]
            igc = gcol[:, h:h + 1]
            igr = grow[h:h + 1, :]
            dmat = jnp.where(causal, fc - fr + igr, -jnp.inf)
            m_prev = m_s[b, h:h + 1, 0:1]
            prior = fc + m_prev
            m_t = jnp.maximum(prior, jnp.max(dmat, axis=1, keepdims=True))
            w_prior = jnp.exp(prior - m_t)
            p = jnp.exp(dmat - m_t)
            qk = lax.dot_general(q, k, NT_DIMS, preferred_element_type=F32)
            s = qk * (p * K_SCALE)
            c_old = c_s[b, h]
            n_old = n_s[b, h:h + 1, :]
            num = (jnp.dot(s.astype(BF16), v, preferred_element_type=F32)
                   + w_prior * jnp.dot(q, c_old.astype(BF16), preferred_element_type=F32))
            den = (jnp.sum(s, axis=1, keepdims=True)
                   + w_prior * jnp.sum(q.astype(F32) * n_old, axis=1, keepdims=True))
            hh = num / jnp.maximum(jnp.abs(den), jnp.exp(-m_t))
            m_new = m_t[CHUNK - 1:CHUNK, :]
            f_last = fc[CHUNK - 1:CHUNK, :]
            a = jnp.exp(f_last + m_prev - m_new)
            ws = jnp.exp(f_last - fc + igc - m_new) * K_SCALE
            kw = k.astype(F32) * ws
            c_s[b, h] = a * c_old + jnp.dot(kw.T.astype(BF16), v, preferred_element_type=F32)
            n_s[b, h:h + 1, :] = a * n_old + jnp.sum(kw, axis=0, keepdims=True)
            m_s[b, h:h + 1, :] = jnp.broadcast_to(m_new, (1, LANES))
            hn = _rms(hh) * mng_ref[:, h * ML_DK:(h + 1) * ML_DK]
            hn = hn * jax.nn.sigmoid(og.astype(F32))
            hm_s[pl.ds(pl.multiple_of(u * CHUNK, CHUNK), CHUNK), h * ML_DK:(h + 1) * ML_DK] = hn
        return carry

    lax.fori_loop(0, nb * nch, unit, 0)

    mix = (jnp.dot(co, wout_ref[0:CONV_W, :], preferred_element_type=F32)
           + jnp.dot(hm_s[...].astype(BF16), wout_ref[CONV_W:CONV_W + ML_W, :], preferred_element_type=F32))
    g1 = mod_ref[:, :, 2 * d:3 * d]
    sh2 = mod_ref[:, :, 3 * d:4 * d]
    sc2 = mod_ref[:, :, 4 * d:5 * d]
    xn = x_ref[...] + g1 * mix.reshape(nb, ts, d)
    xo_ref[...] = xn
    h2 = _rms(xn) * n2g_ref[...] * (1.0 + sc2) + sh2
    h2_ref[...] = _pack_pairs(h2)
    lt_ref[...] = lax.dot_general(rwt_ref[...], h2.reshape(m, d), NT_DIMS, precision=HIGHEST,
                                  preferred_element_type=F32)

    @pl.when(si == pl.num_programs(1) - 1)
    def _store_state():
        convo_ref[...] = last2
        co_ref[...] = c_s[...]
        no_ref[...] = n_s[...]
        mo_ref[...] = m_s[...]


def _mixer(x, mod, pc, pm, gcol, grow, conv0, c0, n0, m0, conv_w, cn_g, mn_g, w_out, n2_g, rw_t):
    batch, seq, d = x.shape
    nb, ts = _seq_tile(batch, seq, ROW_TILE)
    nch = ts // CHUNK
    n_si = seq // ts
    grid = (batch // nb, n_si)
    tile = lambda w: pl.BlockSpec((nb, ts, w), lambda bi, si: (bi, si, 0))
    full = lambda a: pl.BlockSpec(a.shape, lambda bi, si: (0,) * a.ndim)
    per_b = lambda *tail: pl.BlockSpec((nb,) + tail, lambda bi, si: (bi,) + (0,) * len(tail))
    tokens = batch * seq
    return pl.pallas_call(
        _mixer_kernel,
        grid=grid,
        in_specs=[tile(d), per_b(1, 6 * d), tile(3 * CONV_W), tile(4 * ML_W), tile(LANES),
                  pl.BlockSpec((nb, nch, N_GATE, CHUNK), lambda bi, si: (bi, si, 0, 0)),
                  per_b(CONV_K - 1, CONV_W), per_b(ML_H, ML_DK, ML_DK), per_b(8, LANES), per_b(8, LANES),
                  full(conv_w), full(cn_g), full(mn_g), full(w_out), full(n2_g), full(rw_t)],
        out_specs=[tile(d), tile(d // 2),
                   pl.BlockSpec((N_EXPERTS, nb * ts), lambda bi, si: (0, bi * n_si + si)),
                   per_b(CONV_K - 1, CONV_W), per_b(ML_H, ML_DK, ML_DK), per_b(8, LANES), per_b(8, LANES)],
        out_shape=[jax.ShapeDtypeStruct((batch, seq, d), F32),
                   jax.ShapeDtypeStruct((batch, seq, d // 2), U32),
                   jax.ShapeDtypeStruct((N_EXPERTS, tokens), F32),
                   jax.ShapeDtypeStruct((batch, CONV_K - 1, CONV_W), F32),
                   jax.ShapeDtypeStruct((batch, ML_H, ML_DK, ML_DK), F32),
                   jax.ShapeDtypeStruct((batch, 8, LANES), F32),
                   jax.ShapeDtypeStruct((batch, 8, LANES), F32)],
        scratch_shapes=[pltpu.VMEM((nb, ts + 8, CONV_W), F32),
                        pltpu.VMEM((nb, ML_H, ML_DK, ML_DK), F32),
                        pltpu.VMEM((nb, 8, LANES), F32),
                        pltpu.VMEM((nb, 8, LANES), F32),
                        pltpu.VMEM((nb * ts, ML_W), F32)],
        compiler_params=_cparams("parallel", "arbitrary"),
        name="mixer",
    )(x, mod, pc, pm, gcol, grow, conv0, c0, n0, m0, conv_w, cn_g, mn_g, w_out, n2_g, rw_t)


def _route_kernel(lt_ref, rb_ref, tri_ref, wts_ref, lslot_ref, tcnt_ref, tbase_ref, cnt_ref, cnt_s):
    tt = lt_ref.shape[1]

    @pl.when(pl.program_id(0) == 0)
    def _zero():
        cnt_s[...] = jnp.zeros_like(cnt_s)

    neg = -jnp.inf
    scores = jax.nn.sigmoid(lt_ref[...])
    sel3 = (scores + rb_ref[...]).reshape(N_GROUPS, GROUP_SIZE, tt)
    sc3 = scores.reshape(N_GROUPS, GROUP_SIZE, tt)
    shape3 = (N_GROUPS, GROUP_SIZE, tt)
    j3 = lax.broadcasted_iota(I32, shape3, 1).astype(F32)
    g3 = lax.broadcasted_iota(I32, shape3, 0).astype(F32)
    e3 = g3 * GROUP_SIZE + j3
    m1 = jnp.max(sel3, axis=1, keepdims=True)
    first = jnp.min(jnp.where(sel3 == m1, j3, float(GROUP_SIZE)), axis=1, keepdims=True)
    m2 = jnp.max(jnp.where(j3 == first, neg, sel3), axis=1, keepdims=True)
    grp = m1 + m2
    gi = lax.broadcasted_iota(I32, grp.shape, 0).astype(F32)
    gmask = jnp.zeros(grp.shape, F32)
    for _ in range(TOPK_GROUPS):
        gmax = jnp.max(grp, axis=0, keepdims=True)
        gidx = jnp.min(jnp.where(grp == gmax, gi, float(N_GROUPS)), axis=0, keepdims=True)
        hit = gi == gidx
        gmask = jnp.where(hit, 1.0, gmask)
        grp = jnp.where(hit, neg, grp)
    cand = jnp.where(gmask > 0.0, sel3, neg)

    def red(fn, x):
        return fn(fn(x, axis=1, keepdims=True), axis=0, keepdims=True)

    selm = jnp.zeros(shape3, F32)
    idxs, ws = [], []
    for _ in range(TOP_K):
        cmax = red(jnp.max, cand)
        eidx = red(jnp.min, jnp.where(cand == cmax, e3, float(N_EXPERTS)))
        hit = e3 == eidx
        idxs.append(eidx)
        ws.append(red(jnp.sum, jnp.where(hit, sc3, 0.0)))
        selm = jnp.where(hit, 1.0, selm)
        cand = jnp.where(hit, neg, cand)
    wsum = ws[0]
    for k in range(1, TOP_K):
        wsum = wsum + ws[k]
    sel2 = selm.reshape(N_EXPERTS, tt)
    lcum = jnp.dot(sel2.astype(BF16), tri_ref[...], preferred_element_type=F32)
    base = cnt_s[...]
    tcnt = jnp.ceil(jnp.sum(sel2, axis=1, keepdims=True) * (1.0 / ROW_GROUP)) * ROW_GROUP
    tcnt = jnp.broadcast_to(tcnt, (N_EXPERTS, LANES))
    r_e = lax.broadcasted_iota(I32, (N_EXPERTS, N_EXPERTS), 0)
    c_e = lax.broadcasted_iota(I32, (N_EXPERTS, N_EXPERTS), 1)
    loff = jnp.dot((c_e < r_e).astype(BF16), tcnt.astype(BF16), preferred_element_type=F32)
    lsl3 = (lcum + loff[:, 0:1]).reshape(shape3)
    for k in range(TOP_K):
        hit = e3 == idxs[k]
        wts_ref[k:k + 1, :] = (ws[k] / wsum * ROUTED_SCALE).reshape(1, tt)
        lslot_ref[k:k + 1, :] = red(jnp.sum, jnp.where(hit, lsl3, 0.0)).reshape(1, tt).astype(I32)
    tcnt_ref[0] = tcnt
    tbase_ref[0] = base
    cnt_s[...] = base + tcnt
    cnt_ref[...] = cnt_s[...]


def _route(logits_t, router_b):
    tokens = logits_t.shape[1]
    tt = MOE_TILE
    n_tiles = tokens // tt
    r_i = lax.broadcasted_iota(I32, (tt, tt), 0)
    c_i = lax.broadcasted_iota(I32, (tt, tt), 1)
    tri = (r_i < c_i).astype(BF16)
    row = lambda: pl.BlockSpec((TOP_K, tt), lambda i: (0, i))
    stat = lambda: pl.BlockSpec((1, N_EXPERTS, LANES), lambda i: (i, 0, 0))
    return pl.pallas_call(
        _route_kernel,
        grid=(n_tiles,),
        in_specs=[pl.BlockSpec((N_EXPERTS, tt), lambda i: (0, i)),
                  pl.BlockSpec((N_EXPERTS, 1), lambda i: (0, 0)),
                  pl.BlockSpec((tt, tt), lambda i: (0, 0))],
        out_specs=[row(), row(), stat(), stat(),
                   pl.BlockSpec((N_EXPERTS, LANES), lambda i: (0, 0))],
        out_shape=[jax.ShapeDtypeStruct((TOP_K, tokens), F32),
                   jax.ShapeDtypeStruct((TOP_K, tokens), I32),
                   jax.ShapeDtypeStruct((n_tiles, N_EXPERTS, LANES), F32),
                   jax.ShapeDtypeStruct((n_tiles, N_EXPERTS, LANES), F32),
                   jax.ShapeDtypeStruct((N_EXPERTS, LANES), F32)],
        scratch_shapes=[pltpu.VMEM((N_EXPERTS, LANES), F32)],
        compiler_params=_cparams("arbitrary"),
        name="route",
    )(logits_t, router_b.reshape(N_EXPERTS, 1), tri)


MOE_TILE = 256
ROW_GROUP = 8
SLAB = 32
SORT_CHUNK = 512


def _dispatch_kernel(tab_ref, lslot_ref, h_ref, xs_in, xs_hbm, xbuf, pending, sem):
    del xs_in
    tt, dw = h_ref.shape
    i = pl.program_id(0)
    slot = lax.rem(i, 2)
    n_chunks = xbuf.shape[1] // SORT_CHUNK

    lo, hi = _unpack_pairs(h_ref[...])
    lo = lo.astype(BF16)
    hi = hi.astype(BF16)
    lslot = lslot_ref[...]
    for c in range(n_chunks):
        r_i = lax.broadcasted_iota(I32, (SORT_CHUNK, tt), 0) + c * SORT_CHUNK
        perm = jnp.zeros((SORT_CHUNK, tt), F32)
        for k in range(TOP_K):
            perm = jnp.where(r_i == lslot[k:k + 1, :], 1.0, perm)
        perm = perm.astype(BF16)
        x_lo = lax.bitcast_convert_type(jnp.dot(perm, lo, preferred_element_type=F32), U32)
        x_hi = lax.bitcast_convert_type(jnp.dot(perm, hi, preferred_element_type=F32), U32)
        xbuf[slot, c * SORT_CHUNK:(c + 1) * SORT_CHUNK, :] = (x_lo >> 16) | (x_hi & HI_MASK)

    def piece(rows, sl, src, dst):
        return pltpu.make_async_copy(xbuf.at[sl, pl.ds(src, rows)], xs_hbm.at[pl.ds(dst, rows)], sem)

    def drain(n_big, n_small):
        def big(_, carry):
            piece(SLAB, 0, 0, 0).wait()
            return carry

        def small(_, carry):
            piece(ROW_GROUP, 0, 0, 0).wait()
            return carry

        lax.fori_loop(0, n_big, big, 0)
        lax.fori_loop(0, n_small, small, 0)

    @pl.when(i > 0)
    def _wait_previous():
        drain(pending[0], pending[1])

    def per_expert(e, carry):
        dst = pl.multiple_of(tab_ref[0, 0, e], ROW_GROUP)
        src = pl.multiple_of(tab_ref[0, 0, N_EXPERTS + e], ROW_GROUP)
        n_big = tab_ref[0, 0, 2 * N_EXPERTS + e]
        n_small = tab_ref[0, 0, 3 * N_EXPERTS + e]

        def big(s, c2):
            off = pl.multiple_of(s * SLAB, SLAB)
            piece(SLAB, slot, src + off, dst + off).start()
            return c2

        def small(s, c2):
            off = pl.multiple_of(n_big * SLAB + s * ROW_GROUP, ROW_GROUP)
            piece(ROW_GROUP, slot, src + off, dst + off).start()
            return c2

        lax.fori_loop(0, n_big, big, 0)
        lax.fori_loop(0, n_small, small, 0)
        return carry

    lax.fori_loop(0, N_EXPERTS, per_expert, 0)
    pending[0] = tab_ref[0, 0, 4 * N_EXPERTS]
    pending[1] = tab_ref[0, 0, 4 * N_EXPERTS + 1]

    @pl.when(i == pl.num_programs(0) - 1)
    def _wait_last():
        drain(pending[0], pending[1])


def _dispatch(tab, lslot, h2p_flat, rows_total):
    tokens, dw = h2p_flat.shape
    tt = MOE_TILE
    local_rows = pl.cdiv(TOP_K * tt + N_EXPERTS * (ROW_GROUP - 1), SORT_CHUNK) * SORT_CHUNK
    zeros = jnp.zeros((rows_total, dw), h2p_flat.dtype)
    return pl.pallas_call(
        _dispatch_kernel,
        grid=(tokens // tt,),
        in_specs=[pl.BlockSpec((1, 1, tab.shape[2]), lambda i: (i, 0, 0), memory_space=pltpu.SMEM),
                  pl.BlockSpec((TOP_K, tt), lambda i: (0, i)),
                  pl.BlockSpec((tt, dw), lambda i: (i, 0)),
                  pl.BlockSpec(memory_space=pl.ANY)],
        out_specs=pl.BlockSpec(memory_space=pl.ANY),
        out_shape=jax.ShapeDtypeStruct((rows_total, dw), h2p_flat.dtype),
        scratch_shapes=[pltpu.VMEM((2, local_rows, dw), U32), pltpu.SMEM((2,), I32),
                        pltpu.SemaphoreType.DMA],
        input_output_aliases={3: 0},
        compiler_params=_cparams("arbitrary"),
        name="dispatch",
    )(tab, lslot, h2p_flat, zeros)


def _swiglu_packed(xw, w1, w3, w2):
    half = xw.shape[-1]
    lo, hi = _unpack_pairs(xw)
    lo = lo.astype(BF16)
    hi = hi.astype(BF16)
    a = (jnp.dot(lo, w1[:half], preferred_element_type=F32)
         + jnp.dot(hi, w1[half:], preferred_element_type=F32))
    b = (jnp.dot(lo, w3[:half], preferred_element_type=F32)
         + jnp.dot(hi, w3[half:], preferred_element_type=F32))
    hmid = (a * jax.nn.sigmoid(a) * b).astype(BF16)
    return jnp.dot(hmid, w2, preferred_element_type=F32)


def _expert_kernel(be_ref, nu_ref, x_ref, w1_ref, w3_ref, w2_ref, y_ref):
    del be_ref
    j = pl.program_id(0)

    @pl.when(j < nu_ref[0])
    def _compute():
        y_ref[...] = _pack_pairs(_swiglu_packed(x_ref[...], w1_ref[0], w3_ref[0], w2_ref[0]))

    @pl.when(j >= nu_ref[0])
    def _unused():
        y_ref[...] = jnp.zeros_like(y_ref)


def _experts(block_e, n_used, xs, w1, w3, w2, rb):
    rows, dw = xs.shape
    nblk = rows // rb
    d, f = w1.shape[1:]
    return pl.pallas_call(
        _expert_kernel,
        grid_spec=pltpu.PrefetchScalarGridSpec(
            num_scalar_prefetch=2,
            grid=(nblk,),
            in_specs=[pl.BlockSpec((rb, dw), lambda j, be, nu: (jnp.minimum(j, nu[0] - 1), 0)),
                      pl.BlockSpec((1, d, f), lambda j, be, nu: (be[j], 0, 0)),
                      pl.BlockSpec((1, d, f), lambda j, be, nu: (be[j], 0, 0)),
                      pl.BlockSpec((1, f, d), lambda j, be, nu: (be[j], 0, 0))],
            out_specs=pl.BlockSpec((rb, dw), lambda j, be, nu: (j, 0)),
        ),
        out_shape=jax.ShapeDtypeStruct((rows, dw), U32),
        compiler_params=_cparams("arbitrary"),
        name="experts",
    )(block_e, n_used, xs, w1, w3, w2)


def _combine_kernel(tab0_ref, tabc_ref, tabn_ref, y_hbm, lslot_ref, wts_ref, h2_ref, x_ref, mod_ref,
                    sw1_ref, sw3_ref, sw2_ref, fg_ref, out_ref, ybuf, sem, *, final):
    nb, ts, d = x_ref.shape
    tc = nb * ts
    dw = d // 2
    i = pl.program_id(0)
    slot = lax.rem(i, 2)
    other = 1 - slot
    n_chunks = ybuf.shape[1] // SORT_CHUNK

    def piece(rows, sl, src, dst):
        return pltpu.make_async_copy(y_hbm.at[pl.ds(src, rows)], ybuf.at[sl, pl.ds(dst, rows)], sem.at[sl])

    def fetch(tab, sl):
        def per_expert(e, carry):
            src = pl.multiple_of(tab[0, 0, e], ROW_GROUP)
            dst = pl.multiple_of(tab[0, 0, N_EXPERTS + e], ROW_GROUP)
            n_big = tab[0, 0, 2 * N_EXPERTS + e]
            n_small = tab[0, 0, 3 * N_EXPERTS + e]

            def big(s, c2):
                off = pl.multiple_of(s * SLAB, SLAB)
                piece(SLAB, sl, src + off, dst + off).start()
                return c2

            def small(s, c2):
                off = pl.multiple_of(n_big * SLAB + s * ROW_GROUP, ROW_GROUP)
                piece(ROW_GROUP, sl, src + off, dst + off).start()
                return c2

            lax.fori_loop(0, n_big, big, 0)
            lax.fori_loop(0, n_small, small, 0)
            return carry

        lax.fori_loop(0, N_EXPERTS, per_expert, 0)

    def drain(tab, sl):
        def big(_, carry):
            piece(SLAB, sl, 0, 0).wait()
            return carry

        def small(_, carry):
            piece(ROW_GROUP, sl, 0, 0).wait()
            return carry

        lax.fori_loop(0, tab[0, 0, 4 * N_EXPERTS], big, 0)
        lax.fori_loop(0, tab[0, 0, 4 * N_EXPERTS + 1], small, 0)

    @pl.when(i == 0)
    def _first_tile():
        ybuf[...] = jnp.zeros_like(ybuf)
        fetch(tab0_ref, 0)

    fetch(tabn_ref, other)
    shared = _swiglu_packed(h2_ref[...].reshape(tc, dw), sw1_ref[...], sw3_ref[...], sw2_ref[...])
    drain(tabc_ref, slot)

    lslot = lslot_ref[...]
    wts = wts_ref[...]
    acc_lo = shared[:, :dw]
    acc_hi = shared[:, dw:]
    for c in range(n_chunks):
        s_i = lax.broadcasted_iota(I32, (tc, SORT_CHUNK), 1) + c * SORT_CHUNK
        wm = jnp.zeros((tc, SORT_CHUNK), F32)
        for k in range(TOP_K):
            wm = jnp.where(s_i == lslot[:, k:k + 1], wts[:, k:k + 1], wm)
        wm = wm.astype(BF16)
        lo, hi = _unpack_pairs(ybuf[slot, c * SORT_CHUNK:(c + 1) * SORT_CHUNK, :])
        acc_lo = acc_lo + jnp.dot(wm, lo.astype(BF16), preferred_element_type=F32)
        acc_hi = acc_hi + jnp.dot(wm, hi.astype(BF16), preferred_element_type=F32)
    acc = jnp.concatenate([acc_lo, acc_hi], axis=-1)
    xn = x_ref[...] + mod_ref[:, :, 5 * d:6 * d] * acc.reshape(nb, ts, d)
    if final:
        xn = _rms(xn) * fg_ref[...]
    out_ref[...] = xn

    @pl.when(i == pl.num_programs(0) - 1)
    def _last_tile():
        drain(tabn_ref, other)


def _combine(tab, y_sorted, lslot_t, wts_t, h2p, x, mod, sw1, sw3, sw2, final_g, final):
    batch, seq, d = x.shape
    dw = d // 2
    nb, ts = _seq_tile(batch, seq, MOE_TILE)
    tc = nb * ts
    n_si = seq // ts
    n_tiles = (batch // nb) * n_si
    local_rows = pl.cdiv(TOP_K * tc + N_EXPERTS * (ROW_GROUP - 1), SORT_CHUNK) * SORT_CHUNK
    tile = lambda w: pl.BlockSpec((nb, ts, w), lambda i: (i // n_si, i % n_si, 0))
    full = lambda a: pl.BlockSpec(a.shape, lambda i: (0,) * a.ndim)
    tab_spec = lambda f: pl.BlockSpec((1, 1, tab.shape[2]), lambda i: (f(i), 0, 0), memory_space=pltpu.SMEM)
    per_token = lambda: pl.BlockSpec((tc, TOP_K), lambda i: (i, 0))
    return pl.pallas_call(
        functools.partial(_combine_kernel, final=final),
        grid=(n_tiles,),
        in_specs=[tab_spec(lambda i: 0), tab_spec(lambda i: i),
                  tab_spec(lambda i: jnp.minimum(i + 1, n_tiles - 1)),
                  pl.BlockSpec(memory_space=pl.ANY), per_token(), per_token(),
                  tile(dw), tile(d), pl.BlockSpec((nb, 1, 6 * d), lambda i: (i // n_si, 0, 0)),
                  full(sw1), full(sw3), full(sw2), full(final_g)],
        out_specs=tile(d),
        out_shape=jax.ShapeDtypeStruct((batch, seq, d), F32),
        scratch_shapes=[pltpu.VMEM((2, local_rows, dw), U32), pltpu.SemaphoreType.DMA((2,))],
        compiler_params=_cparams("arbitrary"),
        name="combine",
    )(tab, tab, tab, y_sorted, lslot_t, wts_t, h2p, x, mod, sw1, sw3, sw2, final_g)


def _expert_block_rows(tokens):
    if tokens >= 16384:
        return 512
    if tokens >= 2048:
        return 256
    return 64


def _routing_tables(counts, tcnt, tbase, rb, nblk):
    counts = counts.astype(I32)
    padded = (counts + rb - 1) // rb * rb
    pends = jnp.cumsum(padded)
    pstarts = pends - padded
    block_start = jnp.arange(nblk, dtype=I32) * rb
    block_e = jnp.minimum(jnp.sum(pends[None, :] <= block_start[:, None], axis=1), N_EXPERTS - 1).astype(I32)
    n_used = (pends[-1:] // rb).astype(I32)
    tcnt = tcnt.astype(I32)
    first_row = pstarts[None, :] + tbase.astype(I32)
    local_off = jnp.cumsum(tcnt, axis=1) - tcnt
    n_big = tcnt // SLAB
    n_small = (tcnt - n_big * SLAB) // ROW_GROUP
    totals = jnp.stack([jnp.sum(n_big, axis=1), jnp.sum(n_small, axis=1)], axis=1)
    tab = jnp.concatenate([first_row, local_off, n_big, n_small, totals], axis=1)
    tab = jnp.pad(tab, ((0, 0), (0, 5 * N_EXPERTS - tab.shape[1])))[:, None, :]
    return block_e, n_used, tab


def _layer(x, mod, conv0, c0, n0, m0, p, final_g, final):
    batch, seq, d = x.shape
    tokens = batch * seq
    pc, pm, gcol, grow = _inproj(x, mod, p["norm1_g"], p["w_main"], p["wg_col"], p["wg_row"],
                                 p["b_col"], p["b_row"])
    grow = grow.reshape(batch, N_GATE, seq // CHUNK, CHUNK).transpose(0, 2, 1, 3)
    m0b = jnp.broadcast_to(jnp.pad(m0, ((0, 0), (0, 8 - ML_H)))[:, :, None], (batch, 8, LANES))
    n0b = jnp.pad(n0, ((0, 0), (0, 8 - ML_H), (0, 0)))
    x1, h2, logits_t, conv_new, c_new, n_new, m_new = _mixer(
        x, mod, pc, pm, gcol, grow, conv0, c0, n0b, m0b, p["conv_w"], p["conv_norm_g"],
        p["mlstm_norm_g"], p["w_out"], p["norm2_g"], p["router_wt"])
    wts, lslot, tcnt, tbase, cnt = _route(logits_t, p["router_b"])
    rb = _expert_block_rows(tokens)
    n_tiles = tokens // MOE_TILE
    nblk = pl.cdiv(tokens * TOP_K + n_tiles * N_EXPERTS * (ROW_GROUP - 1), rb) + N_EXPERTS
    block_e, n_used, tab = _routing_tables(cnt[:, 0], tcnt[:, :, 0], tbase[:, :, 0], rb, nblk)
    xs = _dispatch(tab, lslot, h2.reshape(tokens, d // 2), nblk * rb)
    ys = _experts(block_e, n_used, xs, p["exp_w1"], p["exp_w3"], p["exp_w2"], rb)
    out = _combine(tab, ys, lslot.T, wts.T, h2, x1, mod, p["shared_w1"], p["shared_w3"], p["shared_w2"],
                   final_g, final)
    return out, conv_new, c_new, n_new[:, :ML_H, :], m_new[:, :ML_H, 0]


def _trunk(x, mods, conv0, c0, n0, m0, layers, final_g):
    depth = len(layers)
    convs, cs, ns, ms = [], [], [], []
    for l in range(depth):
        x, cv, c, n, m = _layer(x, mods[l], conv0[l], c0[l], n0[l], m0[l], layers[l], final_g,
                                final=(l == depth - 1))
        convs.append(cv); cs.append(c); ns.append(n); ms.append(m)
    return x, jnp.stack(convs), jnp.stack(cs), jnp.stack(ns), jnp.stack(ms)


def kernel(x_prompt, x_sample, c_prompt, c_sample, cache_conv, state_mlstm_C, state_mlstm_n, state_mlstm_m,
           norm1_g, ada_w, ada_b, w_in, b_igate, b_fgate, conv_w, conv_norm_g, mlstm_norm_g, w_out, norm2_g,
           router_w, router_b, exp_w1, exp_w3, exp_w2, shared_w1, shared_w3, shared_w2, final_g):
    depth = w_in.shape[0]
    bp = x_prompt.shape[0]
    d = x_prompt.shape[-1]
    mods = _ada(jnp.concatenate([c_prompt, c_sample], axis=0), ada_w, ada_b)
    mods_p = mods[:, :bp, None, :]
    mods_s = mods[:, bp:, None, :]

    w_gate = w_in[:, :, N_MAIN:]
    bias = jnp.concatenate([b_igate, b_fgate], axis=-1)
    layers = []
    for l in range(depth):
        layers.append(dict(
            norm1_g=norm1_g[l][None, :],
            w_main=w_in[l, :, :N_MAIN].astype(BF16),
            wg_col=jnp.pad(w_gate[l], ((0, 0), (0, LANES - N_GATE))),
            wg_row=w_gate[l].T,
            b_col=jnp.pad(bias[l], (0, LANES - N_GATE))[None, :],
            b_row=bias[l][:, None],
            conv_w=conv_w[l], conv_norm_g=conv_norm_g[l][None, :], mlstm_norm_g=mlstm_norm_g[l][None, :],
            w_out=w_out[l].astype(BF16), norm2_g=norm2_g[l][None, :],
            router_wt=router_w[l].T, router_b=router_b[l],
            exp_w1=exp_w1[l].astype(BF16), exp_w3=exp_w3[l].astype(BF16), exp_w2=exp_w2[l].astype(BF16),
            shared_w1=shared_w1[l].astype(BF16), shared_w3=shared_w3[l].astype(BF16),
            shared_w2=shared_w2[l].astype(BF16)))
    fg = final_g[None, :]

    zeros = lambda *s: jnp.zeros((depth, bp) + s, x_prompt.dtype)
    y_p, conv_p, c_p, n_p, m_p = _trunk(
        x_prompt, mods_p, zeros(CONV_K - 1, CONV_W), zeros(ML_H, ML_DK, ML_DK), zeros(ML_H, ML_DK),
        zeros(ML_H), layers, fg)
    y_s, conv_s, c_s, n_s, m_s = _trunk(
        x_sample, mods_s, cache_conv, state_mlstm_C, state_mlstm_n, state_mlstm_m, layers, fg)
    return (y_p, y_s, conv_p, c_p, n_p, m_p, conv_s, c_s, n_s, m_s)
```

```python
import functools

import jax
import jax.numpy as jnp
from jax import lax
from jax.experimental import pallas as pl
from jax.experimental.pallas import tpu as pltpu

F32 = jnp.float32
BF16 = jnp.bfloat16
I32 = jnp.int32
U32 = jnp.uint32
HIGHEST = lax.Precision.HIGHEST

D_MODEL = 1024
CONV_W = 512
CONV_K = 3
ML_W = 512
ML_H = 4
ML_DK = 128
CHUNK = 64
N_EXPERTS = 64
TOP_K = 8
N_GROUPS = 8
GROUP_SIZE = N_EXPERTS // N_GROUPS
TOPK_GROUPS = 4
D_EXPERT = 256
ROUTED_SCALE = 2.5
EPS = 1e-6
K_SCALE = ML_DK ** -0.5
N_MAIN = 3 * CONV_W + 4 * ML_W
N_GATE = 2 * ML_H
LANES = 128
ROW_TILE = 512
VMEM_LIMIT = 56 * 1024 * 1024

NT_DIMS = (((1,), (1,)), ((), ()))


def _cparams(*sem):
    return pltpu.CompilerParams(dimension_semantics=sem, vmem_limit_bytes=VMEM_LIMIT)


def _rms(x):
    return x * lax.rsqrt(jnp.mean(x * x, axis=-1, keepdims=True) + EPS)


def _log_sigmoid(x):
    return jnp.minimum(x, 0.0) - jnp.log1p(jnp.exp(-jnp.abs(x)))


def _pack_pairs(x):
    n = x.shape[-1] // 2
    packed = pltpu.pack_elementwise([x[..., :n], x[..., n:]], packed_dtype=BF16)
    return lax.bitcast_convert_type(packed, U32)


def _unpack_pairs(w):
    w = lax.bitcast_convert_type(w, I32)
    lo = pltpu.unpack_elementwise(w, index=0, packed_dtype=BF16, unpacked_dtype=F32)
    hi = pltpu.unpack_elementwise(w, index=1, packed_dtype=BF16, unpacked_dtype=F32)
    return lo, hi


def _seq_tile(batch, seq, rows):
    if seq >= rows:
        return 1, rows
    return min(batch, rows // seq), seq


def _ada_kernel(c_ref, w_ref, b_ref, o_ref):
    c = c_ref[...]
    s = (c * jax.nn.sigmoid(c)).astype(BF16)
    o_ref[0] = jnp.dot(s, w_ref[0].astype(BF16), preferred_element_type=F32) + b_ref[0]


def _ada(c_all, ada_w, ada_b):
    depth, d, n = ada_w.shape
    nb = c_all.shape[0]
    tn = 1536
    return pl.pallas_call(
        _ada_kernel,
        grid=(depth, n // tn),
        in_specs=[
            pl.BlockSpec((nb, d), lambda l, j: (0, 0)),
            pl.BlockSpec((1, d, tn), lambda l, j: (l, 0, j)),
            pl.BlockSpec((1, 1, tn), lambda l, j: (l, 0, j)),
        ],
        out_specs=pl.BlockSpec((1, nb, tn), lambda l, j: (l, 0, j)),
        out_shape=jax.ShapeDtypeStruct((depth, nb, n), F32),
        compiler_params=_cparams("parallel", "parallel"),
        name="ada",
    )(c_all, ada_w, ada_b.reshape(depth, 1, n))


def _inproj_kernel(x_ref, mod_ref, g_ref, wm_ref, wgc_ref, wgr_ref, bc_ref, br_ref,
                   pc_ref, pm_ref, gc_ref, gr_ref):
    nb, ts, d = x_ref.shape
    m = nb * ts
    sh = mod_ref[:, :, 0:d]
    sc = mod_ref[:, :, d:2 * d]
    h = (_rms(x_ref[...]) * g_ref[...] * (1.0 + sc) + sh).reshape(m, d)
    hb = h.astype(BF16)
    cn = 512
    for j in range(N_MAIN // cn):
        pj = jnp.dot(hb, wm_ref[:, j * cn:(j + 1) * cn], preferred_element_type=F32).astype(BF16)
        if j < 3:
            pc_ref[:, :, j * cn:(j + 1) * cn] = pj.reshape(nb, ts, cn)
        else:
            pm_ref[:, :, (j - 3) * cn:(j - 2) * cn] = pj.reshape(nb, ts, cn)
    gcol = jnp.dot(h, wgc_ref[...], precision=HIGHEST, preferred_element_type=F32) + bc_ref[...]
    lane = lax.broadcasted_iota(I32, gcol.shape, 1)
    gc_ref[...] = jnp.where(lane >= ML_H, _log_sigmoid(gcol), gcol).reshape(nb, ts, LANES)
    grow = lax.dot_general(wgr_ref[...], h, NT_DIMS, precision=HIGHEST,
                           preferred_element_type=F32) + br_ref[...]
    row = lax.broadcasted_iota(I32, grow.shape, 0)
    grow = jnp.where(row >= ML_H, _log_sigmoid(grow), grow)
    for b in range(nb):
        gr_ref[b] = grow[:, b * ts:(b + 1) * ts]


def _inproj(x, mod, norm_g, w_main, wg_col, wg_row, b_col, b_row):
    batch, seq, d = x.shape
    nb, ts = _seq_tile(batch, seq, ROW_TILE)
    grid = (batch // nb, seq // ts)
    tile = lambda w: pl.BlockSpec((nb, ts, w), lambda bi, si: (bi, si, 0))
    full = lambda a: pl.BlockSpec(a.shape, lambda bi, si: (0,) * a.ndim)
    return pl.pallas_call(
        _inproj_kernel,
        grid=grid,
        in_specs=[tile(d), pl.BlockSpec((nb, 1, 6 * d), lambda bi, si: (bi, 0, 0)),
                  full(norm_g), full(w_main), full(wg_col), full(wg_row), full(b_col), full(b_row)],
        out_specs=[tile(3 * CONV_W), tile(4 * ML_W), tile(LANES),
                   pl.BlockSpec((nb, N_GATE, ts), lambda bi, si: (bi, 0, si))],
        out_shape=[jax.ShapeDtypeStruct((batch, seq, 3 * CONV_W), BF16),
                   jax.ShapeDtypeStruct((batch, seq, 4 * ML_W), BF16),
                   jax.ShapeDtypeStruct((batch, seq, LANES), F32),
                   jax.ShapeDtypeStruct((batch, N_GATE, seq), F32)],
        compiler_params=_cparams("parallel", "parallel"),
        name="inproj",
    )(x, mod, norm_g, w_main, wg_col, wg_row, b_col, b_row)


SEQ_PAR = 4


def _mixer_kernel(x_ref, mod_ref, pc_ref, pm_ref, gc_ref, gr_ref, conv0_ref, c0_ref, n0_ref, m0_ref,
                  convw_ref, cng_ref, mng_ref, wout_ref, n2g_ref, rwt_ref,
                  xo_ref, h2_ref, lt_ref, convo_ref, co_ref, no_ref, mo_ref,
                  ubuf, c_s, n_s, m_s, hm_s):
    nb, ts, d = x_ref.shape
    nch = ts // CHUNK
    m = nb * ts
    si = pl.program_id(1)
    hdr = 8

    @pl.when(si == 0)
    def _load_state():
        ubuf[:, hdr - 2:hdr, :] = conv0_ref[...]
        c_s[...] = c0_ref[...]
        n_s[...] = n0_ref[...]
        m_s[...] = m0_ref[...]

    cb = pc_ref[:, :, 0:CONV_W].astype(F32)
    cc = pc_ref[:, :, CONV_W:2 * CONV_W].astype(F32)
    ch = pc_ref[:, :, 2 * CONV_W:3 * CONV_W].astype(F32)
    ubuf[:, hdr:hdr + ts, :] = cc * ch
    w = convw_ref[...]
    y = (w[0:1] * ubuf[:, hdr - 2:hdr - 2 + ts, :] + w[1:2] * ubuf[:, hdr - 1:hdr - 1 + ts, :]
         + w[2:3] * ubuf[:, hdr:hdr + ts, :])
    co = (_rms(cb * y) * cng_ref[...]).reshape(m, CONV_W).astype(BF16)
    last2 = ubuf[:, hdr + ts - 2:hdr + ts, :]
    ubuf[:, hdr - 2:hdr, :] = last2

    r_i = lax.broadcasted_iota(I32, (CHUNK, CHUNK), 0)
    c_i = lax.broadcasted_iota(I32, (CHUNK, CHUNK), 1)
    causal = r_i >= c_i
    ltri = causal.astype(F32)
    utri = (r_i <= c_i).astype(F32)

    def chunk(b, c):
        t0 = pl.multiple_of(c * CHUNK, CHUNK)
        rows = pl.ds(t0, CHUNK)
        out_rows = pl.ds(pl.multiple_of(b * ts + t0, CHUNK), CHUNK)
        gcol = gc_ref[b, rows, :]
        grow = gr_ref[b, c]
        f_col = jnp.dot(ltri, gcol, precision=HIGHEST, preferred_element_type=F32)
        f_row = jnp.dot(grow, utri, precision=HIGHEST, preferred_element_type=F32)
        for h in range(ML_H):
            q = pm_ref[b, rows, h * ML_DK:(h + 1) * ML_DK]
            k = pm_ref[b, rows, ML_W + h * ML_DK:ML_W + (h + 1) * ML_DK]
            v = pm_ref[b, rows, 2 * ML_W + h * ML_DK:2 * ML_W + (h + 1) * ML_DK]
            og = pm_ref[b, rows, 3 * ML_W + h * ML_DK:3 * ML_W + (h + 1) * ML_DK]
            fc = f_col[:, ML_H + h:ML_H + h + 1]
            fr = f_row[ML_H + h:ML_H + h + 1, :]
            igc = gcol[:, h:h + 1]
            igr = grow[h:h + 1, :]
            dmat = jnp.where(causal, fc - fr + igr, -jnp.inf)
            m_prev = m_s[b, h:h + 1, 0:1]
            prior = fc + m_prev
            m_t = jnp.maximum(prior, jnp.max(dmat, axis=1, keepdims=True))
            w_prior = jnp.exp(prior - m_t)
            p = jnp.exp(dmat - m_t)
            qk = lax.dot_general(q, k, NT_DIMS, preferred_element_type=F32)
            s = qk * (p * K_SCALE)
            c_old = c_s[b, h]
            n_old = n_s[b, h:h + 1, :]
            num = (jnp.dot(s.astype(BF16), v, preferred_element_type=F32)
                   + w_prior * jnp.dot(q, c_old.astype(BF16), preferred_element_type=F32))
            den = (jnp.sum(s, axis=1, keepdims=True)
                   + w_prior * jnp.sum(q.astype(F32) * n_old, axis=1, keepdims=True))
            hh = num / jnp.maximum(jnp.abs(den), jnp.exp(-m_t))
            m_new = m_t[CHUNK - 1:CHUNK, :]
            f_last = fc[CHUNK - 1:CHUNK, :]
            a = jnp.exp(f_last + m_prev - m_new)
            ws = jnp.exp(f_last - fc + igc - m_new) * K_SCALE
            kw = k.astype(F32) * ws
            c_s[b, h] = a * c_old + jnp.dot(kw.T.astype(BF16), v, preferred_element_type=F32)
            n_s[b, h:h + 1, :] = a * n_old + jnp.sum(kw, axis=0, keepdims=True)
            m_s[b, h:h + 1, :] = jnp.broadcast_to(m_new, (1, LANES))
            hn = _rms(hh) * mng_ref[:, h * ML_DK:(h + 1) * ML_DK]
            hn = hn * jax.nn.sigmoid(og.astype(F32))
            hm_s[out_rows, h * ML_DK:(h + 1) * ML_DK] = hn

    def unit(u, carry):
        if nch == 1:
            g, c = u, 0
        elif nb == SEQ_PAR:
            g, c = 0, u
        else:
            g, c = u // nch, u % nch
        for j in range(SEQ_PAR):
            chunk(g * SEQ_PAR + j, c)
        return carry

    lax.fori_loop(0, (nb // SEQ_PAR) * nch, unit, 0)

    mix = (jnp.dot(co, wout_ref[0:CONV_W, :], preferred_element_type=F32)
           + jnp.dot(hm_s[...].astype(BF16), wout_ref[CONV_W:CONV_W + ML_W, :], preferred_element_type=F32))
    g1 = mod_ref[:, :, 2 * d:3 * d]
    sh2 = mod_ref[:, :, 3 * d:4 * d]
    sc2 = mod_ref[:, :, 4 * d:5 * d]
    xn = x_ref[...] + g1 * mix.reshape(nb, ts, d)
    xo_ref[...] = xn
    h2 = _rms(xn) * n2g_ref[...] * (1.0 + sc2) + sh2
    h2_ref[...] = _pack_pairs(h2)
    logits_t = lax.dot_general(rwt_ref[...], h2.reshape(m, d), NT_DIMS, precision=HIGHEST,
                               preferred_element_type=F32)
    for b in range(nb):
        lt_ref[b] = logits_t[:, b * ts:(b + 1) * ts]

    @pl.when(si == pl.num_programs(1) - 1)
    def _store_state():
        convo_ref[...] = last2
        co_ref[...] = c_s[...]
        no_ref[...] = n_s[...]
        mo_ref[...] = m_s[...]


def _mixer(x, mod, pc, pm, gcol, grow, conv0, c0, n0, m0, conv_w, cn_g, mn_g, w_out, n2_g, rw_t):
    batch, seq, d = x.shape
    nb, ts = _seq_tile(batch, seq, ROW_TILE // SEQ_PAR)
    nb = min(batch, nb * SEQ_PAR)
    assert nb % SEQ_PAR == 0 and batch % nb == 0
    nch = ts // CHUNK
    n_si = seq // ts
    grid = (batch // nb, n_si)
    tile = lambda w: pl.BlockSpec((nb, ts, w), lambda bi, si: (bi, si, 0))
    full = lambda a: pl.BlockSpec(a.shape, lambda bi, si: (0,) * a.ndim)
    per_b = lambda *tail: pl.BlockSpec((nb,) + tail, lambda bi, si: (bi,) + (0,) * len(tail))
    return pl.pallas_call(
        _mixer_kernel,
        grid=grid,
        in_specs=[tile(d), per_b(1, 6 * d), tile(3 * CONV_W), tile(4 * ML_W), tile(LANES),
                  pl.BlockSpec((nb, nch, N_GATE, CHUNK), lambda bi, si: (bi, si, 0, 0)),
                  per_b(CONV_K - 1, CONV_W), per_b(ML_H, ML_DK, ML_DK), per_b(8, LANES), per_b(8, LANES),
                  full(conv_w), full(cn_g), full(mn_g), full(w_out), full(n2_g), full(rw_t)],
        out_specs=[tile(d), tile(d // 2),
                   pl.BlockSpec((nb, N_EXPERTS, ts), lambda bi, si: (bi, 0, si)),
                   per_b(CONV_K - 1, CONV_W), per_b(ML_H, ML_DK, ML_DK), per_b(8, LANES), per_b(8, LANES)],
        out_shape=[jax.ShapeDtypeStruct((batch, seq, d), F32),
                   jax.ShapeDtypeStruct((batch, seq, d // 2), U32),
                   jax.ShapeDtypeStruct((batch, N_EXPERTS, seq), F32),
                   jax.ShapeDtypeStruct((batch, CONV_K - 1, CONV_W), F32),
                   jax.ShapeDtypeStruct((batch, ML_H, ML_DK, ML_DK), F32),
                   jax.ShapeDtypeStruct((batch, 8, LANES), F32),
                   jax.ShapeDtypeStruct((batch, 8, LANES), F32)],
        scratch_shapes=[pltpu.VMEM((nb, ts + 8, CONV_W), F32),
                        pltpu.VMEM((nb, ML_H, ML_DK, ML_DK), F32),
                        pltpu.VMEM((nb, 8, LANES), F32),
                        pltpu.VMEM((nb, 8, LANES), F32),
                        pltpu.VMEM((nb * ts, ML_W), F32)],
        compiler_params=_cparams("parallel", "arbitrary"),
        name="mixer",
    )(x, mod, pc, pm, gcol, grow, conv0, c0, n0, m0, conv_w, cn_g, mn_g, w_out, n2_g, rw_t)


def _route_kernel(lt_ref, rb_ref, tri_ref, wts_ref, lslot_ref, tcnt_ref, tbase_ref, cnt_ref, cnt_s):
    tt = lt_ref.shape[1]

    @pl.when(pl.program_id(0) == 0)
    def _zero():
        cnt_s[...] = jnp.zeros_like(cnt_s)

    neg = -jnp.inf
    scores = jax.nn.sigmoid(lt_ref[...])
    sel3 = (scores + rb_ref[...]).reshape(N_GROUPS, GROUP_SIZE, tt)
    sc3 = scores.reshape(N_GROUPS, GROUP_SIZE, tt)
    shape3 = (N_GROUPS, GROUP_SIZE, tt)
    j3 = lax.broadcasted_iota(I32, shape3, 1).astype(F32)
    g3 = lax.broadcasted_iota(I32, shape3, 0).astype(F32)
    e3 = g3 * GROUP_SIZE + j3
    m1 = jnp.max(sel3, axis=1, keepdims=True)
    first = jnp.min(jnp.where(sel3 == m1, j3, float(GROUP_SIZE)), axis=1, keepdims=True)
    m2 = jnp.max(jnp.where(j3 == first, neg, sel3), axis=1, keepdims=True)
    grp = m1 + m2
    gi = lax.broadcasted_iota(I32, grp.shape, 0).astype(F32)
    gmask = jnp.zeros(grp.shape, F32)
    for _ in range(TOPK_GROUPS):
        gmax = jnp.max(grp, axis=0, keepdims=True)
        gidx = jnp.min(jnp.where(grp == gmax, gi, float(N_GROUPS)), axis=0, keepdims=True)
        hit = gi == gidx
        gmask = jnp.where(hit, 1.0, gmask)
        grp = jnp.where(hit, neg, grp)
    cand = jnp.where(gmask > 0.0, sel3, neg)

    def red(fn, x):
        return fn(fn(x, axis=1, keepdims=True), axis=0, keepdims=True)

    selm = jnp.zeros(shape3, F32)
    idxs, ws = [], []
    for _ in range(TOP_K):
        cmax = red(jnp.max, cand)
        eidx = red(jnp.min, jnp.where(cand == cmax, e3, float(N_EXPERTS)))
        hit = e3 == eidx
        idxs.append(eidx)
        ws.append(red(jnp.sum, jnp.where(hit, sc3, 0.0)))
        selm = jnp.where(hit, 1.0, selm)
        cand = jnp.where(hit, neg, cand)
    wsum = ws[0]
    for k in range(1, TOP_K):
        wsum = wsum + ws[k]
    sel2 = selm.reshape(N_EXPERTS, tt)
    lcum = jnp.dot(sel2.astype(BF16), tri_ref[...], preferred_element_type=F32)
    base = cnt_s[...]
    tcnt = jnp.ceil(jnp.sum(sel2, axis=1, keepdims=True) * (1.0 / ROW_GROUP)) * ROW_GROUP
    tcnt = jnp.broadcast_to(tcnt, (N_EXPERTS, LANES))
    r_e = lax.broadcasted_iota(I32, (N_EXPERTS, N_EXPERTS), 0)
    c_e = lax.broadcasted_iota(I32, (N_EXPERTS, N_EXPERTS), 1)
    loff = jnp.dot((c_e < r_e).astype(BF16), tcnt.astype(BF16), preferred_element_type=F32)
    lsl3 = (lcum + loff[:, 0:1]).reshape(shape3)
    for k in range(TOP_K):
        hit = e3 == idxs[k]
        wts_ref[k:k + 1, :] = (ws[k] / wsum * ROUTED_SCALE).reshape(1, tt)
        lslot_ref[k:k + 1, :] = red(jnp.sum, jnp.where(hit, lsl3, 0.0)).reshape(1, tt).astype(I32)
    tcnt_ref[0] = tcnt
    tbase_ref[0] = base
    cnt_s[...] = base + tcnt
    cnt_ref[...] = cnt_s[...]


def _route(logits_t, router_b):
    tokens = logits_t.shape[1]
    tt = MOE_TILE
    n_tiles = tokens // tt
    r_i = lax.broadcasted_iota(I32, (tt, tt), 0)
    c_i = lax.broadcasted_iota(I32, (tt, tt), 1)
    tri = (r_i < c_i).astype(BF16)
    row = lambda: pl.BlockSpec((TOP_K, tt), lambda i: (0, i))
    stat = lambda: pl.BlockSpec((1, N_EXPERTS, LANES), lambda i: (i, 0, 0))
    return pl.pallas_call(
        _route_kernel,
        grid=(n_tiles,),
        in_specs=[pl.BlockSpec((N_EXPERTS, tt), lambda i: (0, i)),
                  pl.BlockSpec((N_EXPERTS, 1), lambda i: (0, 0)),
                  pl.BlockSpec((tt, tt), lambda i: (0, 0))],
        out_specs=[row(), row(), stat(), stat(),
                   pl.BlockSpec((N_EXPERTS, LANES), lambda i: (0, 0))],
        out_shape=[jax.ShapeDtypeStruct((TOP_K, tokens), F32),
                   jax.ShapeDtypeStruct((TOP_K, tokens), I32),
                   jax.ShapeDtypeStruct((n_tiles, N_EXPERTS, LANES), F32),
                   jax.ShapeDtypeStruct((n_tiles, N_EXPERTS, LANES), F32),
                   jax.ShapeDtypeStruct((N_EXPERTS, LANES), F32)],
        scratch_shapes=[pltpu.VMEM((N_EXPERTS, LANES), F32)],
        compiler_params=_cparams("arbitrary"),
        name="route",
    )(logits_t, router_b.reshape(N_EXPERTS, 1), tri)


MOE_TILE = 256
ROW_GROUP = 8
SLAB = 32
SORT_CHUNK = 512
FILL_ROWS = 64


def _dispatch_kernel(fill_ref, tab_ref, lslot_ref, h_ref, xs_hbm, xbuf, zbuf, pending, sem, fill_sem):
    tt, dw = h_ref.shape
    i = pl.program_id(0)
    slot = lax.rem(i, 2)
    n_chunks = xbuf.shape[1] // SORT_CHUNK

    @pl.when(i == 0)
    def _zero_fill():
        zbuf[...] = jnp.zeros_like(zbuf)

        def zero_piece(rows, dst):
            return pltpu.make_async_copy(zbuf.at[pl.ds(0, rows)], xs_hbm.at[pl.ds(dst, rows)], fill_sem)

        def run(n, rows, start, do_start):
            def body(s, carry):
                cp = zero_piece(rows, start + pl.multiple_of(s * rows, rows))
                if do_start:
                    cp.start()
                else:
                    cp.wait()
                return carry
            lax.fori_loop(0, n, body, 0)

        def per_expert(e, carry):
            start = pl.multiple_of(fill_ref[0, 0, e], ROW_GROUP)
            n_big = fill_ref[0, 0, N_EXPERTS + e]
            run(n_big, SLAB, start, True)
            run(fill_ref[0, 0, 2 * N_EXPERTS + e], ROW_GROUP,
                start + pl.multiple_of(n_big * SLAB, SLAB), True)
            return carry

        lax.fori_loop(0, N_EXPERTS, per_expert, 0)
        n_tail = fill_ref[0, 0, 3 * N_EXPERTS + 1]
        run(n_tail, FILL_ROWS, pl.multiple_of(fill_ref[0, 0, 3 * N_EXPERTS], FILL_ROWS), True)
        run(fill_ref[0, 0, 3 * N_EXPERTS + 2], SLAB, 0, False)
        run(fill_ref[0, 0, 3 * N_EXPERTS + 3], ROW_GROUP, 0, False)
        run(n_tail, FILL_ROWS, 0, False)

    lo, hi = _unpack_pairs(h_ref[...])
    lo = lo.astype(BF16)
    hi = hi.astype(BF16)
    lslot = lslot_ref[...]
    for c in range(n_chunks):
        r_i = lax.broadcasted_iota(I32, (SORT_CHUNK, tt), 0) + c * SORT_CHUNK
        perm = jnp.zeros((SORT_CHUNK, tt), F32)
        for k in range(TOP_K):
            perm = jnp.where(r_i == lslot[k:k + 1, :], 1.0, perm)
        perm = perm.astype(BF16)
        x_lo = jnp.dot(perm, lo, preferred_element_type=F32)
        x_hi = jnp.dot(perm, hi, preferred_element_type=F32)
        xbuf[slot, c * SORT_CHUNK:(c + 1) * SORT_CHUNK, :] = _pack_pairs(
            jnp.concatenate([x_lo, x_hi], axis=-1))

    def piece(rows, sl, src, dst):
        return pltpu.make_async_copy(xbuf.at[sl, pl.ds(src, rows)], xs_hbm.at[pl.ds(dst, rows)], sem)

    def drain(n_big, n_small):
        def big(_, carry):
            piece(SLAB, 0, 0, 0).wait()
            return carry

        def small(_, carry):
            piece(ROW_GROUP, 0, 0, 0).wait()
            return carry

        lax.fori_loop(0, n_big, big, 0)
        lax.fori_loop(0, n_small, small, 0)

    @pl.when(i > 0)
    def _wait_previous():
        drain(pending[0], pending[1])

    def per_expert(e, carry):
        dst = pl.multiple_of(tab_ref[0, 0, e], ROW_GROUP)
        src = pl.multiple_of(tab_ref[0, 0, N_EXPERTS + e], ROW_GROUP)
        n_big = tab_ref[0, 0, 2 * N_EXPERTS + e]
        n_small = tab_ref[0, 0, 3 * N_EXPERTS + e]

        def big(s, c2):
            off = pl.multiple_of(s * SLAB, SLAB)
            piece(SLAB, slot, src + off, dst + off).start()
            return c2

        def small(s, c2):
            off = pl.multiple_of(n_big * SLAB + s * ROW_GROUP, ROW_GROUP)
            piece(ROW_GROUP, slot, src + off, dst + off).start()
            return c2

        lax.fori_loop(0, n_big, big, 0)
        lax.fori_loop(0, n_small, small, 0)
        return carry

    lax.fori_loop(0, N_EXPERTS, per_expert, 0)
    pending[0] = tab_ref[0, 0, 4 * N_EXPERTS]
    pending[1] = tab_ref[0, 0, 4 * N_EXPERTS + 1]

    @pl.when(i == pl.num_programs(0) - 1)
    def _wait_last():
        drain(pending[0], pending[1])


def _dispatch(fill, tab, lslot, h2p_flat, rows_total):
    tokens, dw = h2p_flat.shape
    tt = MOE_TILE
    local_rows = pl.cdiv(TOP_K * tt + N_EXPERTS * (ROW_GROUP - 1), SORT_CHUNK) * SORT_CHUNK
    return pl.pallas_call(
        _dispatch_kernel,
        grid=(tokens // tt,),
        in_specs=[pl.BlockSpec(fill.shape, lambda i: (0, 0, 0), memory_space=pltpu.SMEM),
                  pl.BlockSpec((1, 1, tab.shape[2]), lambda i: (i, 0, 0), memory_space=pltpu.SMEM),
                  pl.BlockSpec((TOP_K, tt), lambda i: (0, i)),
                  pl.BlockSpec((tt, dw), lambda i: (i, 0))],
        out_specs=pl.BlockSpec(memory_space=pl.ANY),
        out_shape=jax.ShapeDtypeStruct((rows_total, dw), h2p_flat.dtype),
        scratch_shapes=[pltpu.VMEM((2, local_rows, dw), U32), pltpu.VMEM((FILL_ROWS, dw), U32),
                        pltpu.SMEM((2,), I32), pltpu.SemaphoreType.DMA, pltpu.SemaphoreType.DMA],
        compiler_params=_cparams("arbitrary"),
        name="dispatch",
    )(fill, tab, lslot, h2p_flat)


def _swiglu_packed(xw, w1, w3, w2):
    half = xw.shape[-1]
    lo, hi = _unpack_pairs(xw)
    lo = lo.astype(BF16)
    hi = hi.astype(BF16)
    a = (jnp.dot(lo, w1[:half], preferred_element_type=F32)
         + jnp.dot(hi, w1[half:], preferred_element_type=F32))
    b = (jnp.dot(lo, w3[:half], preferred_element_type=F32)
         + jnp.dot(hi, w3[half:], preferred_element_type=F32))
    hmid = (a * jax.nn.sigmoid(a) * b).astype(BF16)
    return jnp.dot(hmid, w2, preferred_element_type=F32)


def _expert_kernel(be_ref, nu_ref, x_ref, w1_ref, w3_ref, w2_ref, y_ref):
    del be_ref
    j = pl.program_id(0)

    @pl.when(j < nu_ref[0])
    def _compute():
        y_ref[...] = _pack_pairs(_swiglu_packed(x_ref[...], w1_ref[0], w3_ref[0], w2_ref[0]))

    @pl.when(j >= nu_ref[0])
    def _unused():
        y_ref[...] = jnp.zeros_like(y_ref)


def _experts(block_e, n_used, xs, w1, w3, w2, rb):
    rows, dw = xs.shape
    nblk = rows // rb
    d, f = w1.shape[1:]
    return pl.pallas_call(
        _expert_kernel,
        grid_spec=pltpu.PrefetchScalarGridSpec(
            num_scalar_prefetch=2,
            grid=(nblk,),
            in_specs=[pl.BlockSpec((rb, dw), lambda j, be, nu: (jnp.minimum(j, nu[0] - 1), 0)),
                      pl.BlockSpec((1, d, f), lambda j, be, nu: (be[j], 0, 0)),
                      pl.BlockSpec((1, d, f), lambda j, be, nu: (be[j], 0, 0)),
                      pl.BlockSpec((1, f, d), lambda j, be, nu: (be[j], 0, 0))],
            out_specs=pl.BlockSpec((rb, dw), lambda j, be, nu: (j, 0)),
        ),
        out_shape=jax.ShapeDtypeStruct((rows, dw), U32),
        compiler_params=_cparams("arbitrary"),
        name="experts",
    )(block_e, n_used, xs, w1, w3, w2)


def _combine_kernel(tab0_ref, tabc_ref, tabn_ref, y_hbm, lslot_ref, wts_ref, h2_ref, x_ref, mod_ref,
                    sw1_ref, sw3_ref, sw2_ref, fg_ref, out_ref, ybuf, sem, *, final):
    nb, ts, d = x_ref.shape
    tc = nb * ts
    dw = d // 2
    i = pl.program_id(0)
    slot = lax.rem(i, 2)
    other = 1 - slot
    n_chunks = ybuf.shape[1] // SORT_CHUNK

    def piece(rows, sl, src, dst):
        return pltpu.make_async_copy(y_hbm.at[pl.ds(src, rows)], ybuf.at[sl, pl.ds(dst, rows)], sem.at[sl])

    def fetch(tab, sl):
        def per_expert(e, carry):
            src = pl.multiple_of(tab[0, 0, e], ROW_GROUP)
            dst = pl.multiple_of(tab[0, 0, N_EXPERTS + e], ROW_GROUP)
            n_big = tab[0, 0, 2 * N_EXPERTS + e]
            n_small = tab[0, 0, 3 * N_EXPERTS + e]

            def big(s, c2):
                off = pl.multiple_of(s * SLAB, SLAB)
                piece(SLAB, sl, src + off, dst + off).start()
                return c2

            def small(s, c2):
                off = pl.multiple_of(n_big * SLAB + s * ROW_GROUP, ROW_GROUP)
                piece(ROW_GROUP, sl, src + off, dst + off).start()
                return c2

            lax.fori_loop(0, n_big, big, 0)
            lax.fori_loop(0, n_small, small, 0)
            return carry

        lax.fori_loop(0, N_EXPERTS, per_expert, 0)

    def drain(tab, sl):
        def big(_, carry):
            piece(SLAB, sl, 0, 0).wait()
            return carry

        def small(_, carry):
            piece(ROW_GROUP, sl, 0, 0).wait()
            return carry

        lax.fori_loop(0, tab[0, 0, 4 * N_EXPERTS], big, 0)
        lax.fori_loop(0, tab[0, 0, 4 * N_EXPERTS + 1], small, 0)

    @pl.when(i == 0)
    def _first_tile():
        ybuf[...] = jnp.zeros_like(ybuf)
        fetch(tab0_ref, 0)

    fetch(tabn_ref, other)
    shared = _swiglu_packed(h2_ref[...].reshape(tc, dw), sw1_ref[...], sw3_ref[...], sw2_ref[...])
    drain(tabc_ref, slot)

    lslot = lslot_ref[...]
    wts = wts_ref[...]
    acc_lo = shared[:, :dw]
    acc_hi = shared[:, dw:]
    for c in range(n_chunks):
        s_i = lax.broadcasted_iota(I32, (tc, SORT_CHUNK), 1) + c * SORT_CHUNK
        wm = jnp.zeros((tc, SORT_CHUNK), F32)
        for k in range(TOP_K):
            wm = jnp.where(s_i == lslot[:, k:k + 1], wts[:, k:k + 1], wm)
        wm = wm.astype(BF16)
        lo, hi = _unpack_pairs(ybuf[slot, c * SORT_CHUNK:(c + 1) * SORT_CHUNK, :])
        acc_lo = acc_lo + jnp.dot(wm, lo.astype(BF16), preferred_element_type=F32)
        acc_hi = acc_hi + jnp.dot(wm, hi.astype(BF16), preferred_element_type=F32)
    acc = jnp.concatenate([acc_lo, acc_hi], axis=-1)
    xn = x_ref[...] + mod_ref[:, :, 5 * d:6 * d] * acc.reshape(nb, ts, d)
    if final:
        xn = _rms(xn) * fg_ref[...]
    out_ref[...] = xn

    @pl.when(i == pl.num_programs(0) - 1)
    def _last_tile():
        drain(tabn_ref, other)


def _combine(tab, y_sorted, lslot_t, wts_t, h2p, x, mod, sw1, sw3, sw2, final_g, final):
    batch, seq, d = x.shape
    dw = d // 2
    nb, ts = _seq_tile(batch, seq, MOE_TILE)
    tc = nb * ts
    n_si = seq // ts
    n_tiles = (batch // nb) * n_si
    local_rows = pl.cdiv(TOP_K * tc + N_EXPERTS * (ROW_GROUP - 1), SORT_CHUNK) * SORT_CHUNK
    tile = lambda w: pl.BlockSpec((nb, ts, w), lambda i: (i // n_si, i % n_si, 0))
    full = lambda a: pl.BlockSpec(a.shape, lambda i: (0,) * a.ndim)
    tab_spec = lambda f: pl.BlockSpec((1, 1, tab.shape[2]), lambda i: (f(i), 0, 0), memory_space=pltpu.SMEM)
    per_token = lambda: pl.BlockSpec((tc, TOP_K), lambda i: (i, 0))
    return pl.pallas_call(
        functools.partial(_combine_kernel, final=final),
        grid=(n_tiles,),
        in_specs=[tab_spec(lambda i: 0), tab_spec(lambda i: i),
                  tab_spec(lambda i: jnp.minimum(i + 1, n_tiles - 1)),
                  pl.BlockSpec(memory_space=pl.ANY), per_token(), per_token(),
                  tile(dw), tile(d), pl.BlockSpec((nb, 1, 6 * d), lambda i: (i // n_si, 0, 0)),
                  full(sw1), full(sw3), full(sw2), full(final_g)],
        out_specs=tile(d),
        out_shape=jax.ShapeDtypeStruct((batch, seq, d), F32),
        scratch_shapes=[pltpu.VMEM((2, local_rows, dw), U32), pltpu.SemaphoreType.DMA((2,))],
        compiler_params=_cparams("arbitrary"),
        name="combine",
    )(tab, tab, tab, y_sorted, lslot_t, wts_t, h2p, x, mod, sw1, sw3, sw2, final_g)


def _expert_block_rows(tokens):
    if tokens >= 16384:
        return 512
    if tokens >= 2048:
        return 256
    return 64


def _routing_tables(counts, tcnt, tbase, rb, nblk):
    counts = counts.astype(I32)
    padded = (counts + rb - 1) // rb * rb
    pends = jnp.cumsum(padded)
    pstarts = pends - padded
    block_start = jnp.arange(nblk, dtype=I32) * rb
    block_e = jnp.minimum(jnp.sum(pends[None, :] <= block_start[:, None], axis=1), N_EXPERTS - 1).astype(I32)
    n_used = (pends[-1:] // rb).astype(I32)
    tcnt = tcnt.astype(I32)
    first_row = pstarts[None, :] + tbase.astype(I32)
    local_off = jnp.cumsum(tcnt, axis=1) - tcnt
    n_big = tcnt // SLAB
    n_small = (tcnt - n_big * SLAB) // ROW_GROUP
    totals = jnp.stack([jnp.sum(n_big, axis=1), jnp.sum(n_small, axis=1)], axis=1)
    tab = jnp.concatenate([first_row, local_off, n_big, n_small, totals], axis=1)
    tab = jnp.pad(tab, ((0, 0), (0, 5 * N_EXPERTS - tab.shape[1])))[:, None, :]
    pad_rows = padded - counts
    pad_big = pad_rows // SLAB
    pad_small = (pad_rows - pad_big * SLAB) // ROW_GROUP
    tail = jnp.stack([pends[-1], (nblk * rb - pends[-1]) // FILL_ROWS, jnp.sum(pad_big), jnp.sum(pad_small)])
    fill = jnp.concatenate([pstarts + counts, pad_big, pad_small, tail])
    fill = jnp.pad(fill, (0, 4 * N_EXPERTS - fill.shape[0]))[None, None, :]
    return block_e, n_used, tab, fill


def _layer(x, mod, conv0, c0, n0, m0, p, final_g, final):
    batch, seq, d = x.shape
    tokens = batch * seq
    pc, pm, gcol, grow = _inproj(x, mod, p["norm1_g"], p["w_main"], p["wg_col"], p["wg_row"],
                                 p["b_col"], p["b_row"])
    grow = grow.reshape(batch, N_GATE, seq // CHUNK, CHUNK).transpose(0, 2, 1, 3)
    m0b = jnp.broadcast_to(jnp.pad(m0, ((0, 0), (0, 8 - ML_H)))[:, :, None], (batch, 8, LANES))
    n0b = jnp.pad(n0, ((0, 0), (0, 8 - ML_H), (0, 0)))
    x1, h2, logits_t, conv_new, c_new, n_new, m_new = _mixer(
        x, mod, pc, pm, gcol, grow, conv0, c0, n0b, m0b, p["conv_w"], p["conv_norm_g"],
        p["mlstm_norm_g"], p["w_out"], p["norm2_g"], p["router_wt"])
    logits_t = logits_t.transpose(1, 0, 2).reshape(N_EXPERTS, tokens)
    wts, lslot, tcnt, tbase, cnt = _route(logits_t, p["router_b"])
    rb = _expert_block_rows(tokens)
    n_tiles = tokens // MOE_TILE
    nblk = pl.cdiv(tokens * TOP_K + n_tiles * N_EXPERTS * (ROW_GROUP - 1), rb) + N_EXPERTS
    block_e, n_used, tab, fill = _routing_tables(cnt[:, 0], tcnt[:, :, 0], tbase[:, :, 0], rb, nblk)
    xs = _dispatch(fill, tab, lslot, h2.reshape(tokens, d // 2), nblk * rb)
    ys = _experts(block_e, n_used, xs, p["exp_w1"], p["exp_w3"], p["exp_w2"], rb)
    out = _combine(tab, ys, lslot.T, wts.T, h2, x1, mod, p["shared_w1"], p["shared_w3"], p["shared_w2"],
                   final_g, final)
    return out, conv_new, c_new, n_new[:, :ML_H, :], m_new[:, :ML_H, 0]


def _trunk(x, mods, conv0, c0, n0, m0, layers, final_g):
    depth = len(layers)
    convs, cs, ns, ms = [], [], [], []
    for l in range(depth):
        x, cv, c, n, m = _layer(x, mods[l], conv0[l], c0[l], n0[l], m0[l], layers[l], final_g,
                                final=(l == depth - 1))
        convs.append(cv); cs.append(c); ns.append(n); ms.append(m)
    return x, jnp.stack(convs), jnp.stack(cs), jnp.stack(ns), jnp.stack(ms)


def kernel(x_prompt, x_sample, c_prompt, c_sample, cache_conv, state_mlstm_C, state_mlstm_n, state_mlstm_m,
           norm1_g, ada_w, ada_b, w_in, b_igate, b_fgate, conv_w, conv_norm_g, mlstm_norm_g, w_out, norm2_g,
           router_w, router_b, exp_w1, exp_w3, exp_w2, shared_w1, shared_w3, shared_w2, final_g):
    depth = w_in.shape[0]
    bp = x_prompt.shape[0]
    d = x_prompt.shape[-1]
    mods = _ada(jnp.concatenate([c_prompt, c_sample], axis=0), ada_w, ada_b)
    mods_p = mods[:, :bp, None, :]
    mods_s = mods[:, bp:, None, :]

    w_gate = w_in[:, :, N_MAIN:]
    bias = jnp.concatenate([b_igate, b_fgate], axis=-1)
    layers = []
    for l in range(depth):
        layers.append(dict(
            norm1_g=norm1_g[l][None, :],
            w_main=w_in[l, :, :N_MAIN].astype(BF16),
            wg_col=jnp.pad(w_gate[l], ((0, 0), (0, LANES - N_GATE))),
            wg_row=w_gate[l].T,
            b_col=jnp.pad(bias[l], (0, LANES - N_GATE))[None, :],
            b_row=bias[l][:, None],
            conv_w=conv_w[l], conv_norm_g=conv_norm_g[l][None, :], mlstm_norm_g=mlstm_norm_g[l][None, :],
            w_out=w_out[l].astype(BF16), norm2_g=norm2_g[l][None, :],
            router_wt=router_w[l].T, router_b=router_b[l],
            exp_w1=exp_w1[l].astype(BF16), exp_w3=exp_w3[l].astype(BF16), exp_w2=exp_w2[l].astype(BF16),
            shared_w1=shared_w1[l].astype(BF16), shared_w3=shared_w3[l].astype(BF16),
            shared_w2=shared_w2[l].astype(BF16)))
    fg = final_g[None, :]

    zeros = lambda *s: jnp.zeros((depth, bp) + s, x_prompt.dtype)
    y_p, conv_p, c_p, n_p, m_p = _trunk(
        x_prompt, mods_p, zeros(CONV_K - 1, CONV_W), zeros(ML_H, ML_DK, ML_DK), zeros(ML_H, ML_DK),
        zeros(ML_H), layers, fg)
    y_s, conv_s, c_s, n_s, m_s = _trunk(
        x_sample, mods_s, cache_conv, state_mlstm_C, state_mlstm_n, state_mlstm_m, layers, fg)
    return (y_p, y_s, conv_p, c_p, n_p, m_p, conv_s, c_s, n_s, m_s)
```

```python
import functools

import jax
import jax.numpy as jnp
from jax import lax
from jax.experimental import pallas as pl
from jax.experimental.pallas import tpu as pltpu

F32 = jnp.float32
BF16 = jnp.bfloat16
I32 = jnp.int32
U32 = jnp.uint32
HIGHEST = lax.Precision.HIGHEST

D_MODEL = 1024
CONV_W = 512
CONV_K = 3
ML_W = 512
ML_H = 4
ML_DK = 128
CHUNK = 64
N_EXPERTS = 64
TOP_K = 8
N_GROUPS = 8
GROUP_SIZE = N_EXPERTS // N_GROUPS
TOPK_GROUPS = 4
D_EXPERT = 256
ROUTED_SCALE = 2.5
EPS = 1e-6
K_SCALE = ML_DK ** -0.5
N_MAIN = 3 * CONV_W + 4 * ML_W
N_GATE = 2 * ML_H
LANES = 128
ROW_TILE = 512
VMEM_LIMIT = 56 * 1024 * 1024

NT_DIMS = (((1,), (1,)), ((), ()))


def _cparams(*sem):
    return pltpu.CompilerParams(dimension_semantics=sem, vmem_limit_bytes=VMEM_LIMIT)


def _rms(x):
    return x * lax.rsqrt(jnp.mean(x * x, axis=-1, keepdims=True) + EPS)


def _dot3(a, b, dims=None):
    def split(x):
        hi = x.astype(BF16)
        return hi, (x - hi.astype(F32)).astype(BF16)

    def dot(x, y):
        if dims is None:
            return jnp.dot(x, y, preferred_element_type=F32)
        return lax.dot_general(x, y, dims, preferred_element_type=F32)

    a_hi, a_lo = split(a)
    b_hi, b_lo = split(b)
    return dot(a_hi, b_hi) + dot(a_hi, b_lo) + dot(a_lo, b_hi)


def _log_sigmoid(x):
    return jnp.minimum(x, 0.0) - jnp.log1p(jnp.exp(-jnp.abs(x)))


def _pack_pairs(x):
    n = x.shape[-1] // 2
    packed = pltpu.pack_elementwise([x[..., :n], x[..., n:]], packed_dtype=BF16)
    return lax.bitcast_convert_type(packed, U32)


def _unpack_pairs(w):
    w = lax.bitcast_convert_type(w, I32)
    lo = pltpu.unpack_elementwise(w, index=0, packed_dtype=BF16, unpacked_dtype=F32)
    hi = pltpu.unpack_elementwise(w, index=1, packed_dtype=BF16, unpacked_dtype=F32)
    return lo, hi


def _seq_tile(batch, seq, rows):
    if seq >= rows:
        return 1, rows
    return min(batch, rows // seq), seq


def _ada_kernel(c_ref, w_ref, b_ref, o_ref):
    c = c_ref[...]
    s = (c * jax.nn.sigmoid(c)).astype(BF16)
    o_ref[0] = jnp.dot(s, w_ref[0].astype(BF16), preferred_element_type=F32) + b_ref[0]


def _ada(c_all, ada_w, ada_b):
    depth, d, n = ada_w.shape
    nb = c_all.shape[0]
    tn = 1536
    return pl.pallas_call(
        _ada_kernel,
        grid=(depth, n // tn),
        in_specs=[
            pl.BlockSpec((nb, d), lambda l, j: (0, 0)),
            pl.BlockSpec((1, d, tn), lambda l, j: (l, 0, j)),
            pl.BlockSpec((1, 1, tn), lambda l, j: (l, 0, j)),
        ],
        out_specs=pl.BlockSpec((1, nb, tn), lambda l, j: (l, 0, j)),
        out_shape=jax.ShapeDtypeStruct((depth, nb, n), F32),
        compiler_params=_cparams("parallel", "parallel"),
        name="ada",
    )(c_all, ada_w, ada_b.reshape(depth, 1, n))


def _inproj_kernel(x_ref, mod_ref, g_ref, wm_ref, wgc_ref, bc_ref, pc_ref, pm_ref, gc_ref, gr_ref):
    nb, ts, d = x_ref.shape
    m = nb * ts
    sh = mod_ref[:, :, 0:d]
    sc = mod_ref[:, :, d:2 * d]
    h = (_rms(x_ref[...]) * g_ref[...] * (1.0 + sc) + sh).reshape(m, d)
    hb = h.astype(BF16)
    cn = 512
    for j in range(N_MAIN // cn):
        pj = jnp.dot(hb, wm_ref[:, j * cn:(j + 1) * cn], preferred_element_type=F32).astype(BF16)
        if j < 3:
            pc_ref[:, :, j * cn:(j + 1) * cn] = pj.reshape(nb, ts, cn)
        else:
            pm_ref[:, :, (j - 3) * cn:(j - 2) * cn] = pj.reshape(nb, ts, cn)
    gcol = _dot3(h, wgc_ref[...]) + bc_ref[...]
    lane = lax.broadcasted_iota(I32, gcol.shape, 1)
    gcol = jnp.where(lane >= ML_H, _log_sigmoid(gcol), gcol)
    gc_ref[...] = gcol.reshape(nb, ts, LANES)
    grow = gcol.T[:N_GATE]
    for b in range(nb):
        gr_ref[b] = grow[:, b * ts:(b + 1) * ts]


def _inproj(x, mod, norm_g, w_main, wg_col, b_col):
    batch, seq, d = x.shape
    nb, ts = _seq_tile(batch, seq, ROW_TILE)
    grid = (batch // nb, seq // ts)
    tile = lambda w: pl.BlockSpec((nb, ts, w), lambda bi, si: (bi, si, 0))
    full = lambda a: pl.BlockSpec(a.shape, lambda bi, si: (0,) * a.ndim)
    return pl.pallas_call(
        _inproj_kernel,
        grid=grid,
        in_specs=[tile(d), pl.BlockSpec((nb, 1, 6 * d), lambda bi, si: (bi, 0, 0)),
                  full(norm_g), full(w_main), full(wg_col), full(b_col)],
        out_specs=[tile(3 * CONV_W), tile(4 * ML_W), tile(LANES),
                   pl.BlockSpec((nb, N_GATE, ts), lambda bi, si: (bi, 0, si))],
        out_shape=[jax.ShapeDtypeStruct((batch, seq, 3 * CONV_W), BF16),
                   jax.ShapeDtypeStruct((batch, seq, 4 * ML_W), BF16),
                   jax.ShapeDtypeStruct((batch, seq, LANES), F32),
                   jax.ShapeDtypeStruct((batch, N_GATE, seq), F32)],
        compiler_params=_cparams("parallel", "parallel"),
        name="inproj",
    )(x, mod, norm_g, w_main, wg_col, b_col)


SEQ_PAR = 4


def _mixer_kernel(x_ref, mod_ref, pc_ref, pm_ref, gc_ref, gr_ref, conv0_ref, c0_ref, n0_ref, m0_ref,
                  convw_ref, cng_ref, mng_ref, wout_ref, n2g_ref, rwt_ref,
                  xo_ref, h2_ref, lt_ref, convo_ref, co_ref, no_ref, mo_ref,
                  ubuf, c_s, n_s, m_s, hm_s):
    nb, ts, d = x_ref.shape
    nch = ts // CHUNK
    m = nb * ts
    si = pl.program_id(1)
    hdr = 8

    @pl.when(si == 0)
    def _load_state():
        ubuf[:, hdr - 2:hdr, :] = conv0_ref[...]
        c_s[...] = c0_ref[...]
        n_s[...] = n0_ref[...]
        m_s[...] = m0_ref[...]

    cb = pc_ref[:, :, 0:CONV_W].astype(F32)
    cc = pc_ref[:, :, CONV_W:2 * CONV_W].astype(F32)
    ch = pc_ref[:, :, 2 * CONV_W:3 * CONV_W].astype(F32)
    ubuf[:, hdr:hdr + ts, :] = cc * ch
    w = convw_ref[...]
    y = (w[0:1] * ubuf[:, hdr - 2:hdr - 2 + ts, :] + w[1:2] * ubuf[:, hdr - 1:hdr - 1 + ts, :]
         + w[2:3] * ubuf[:, hdr:hdr + ts, :])
    co = (_rms(cb * y) * cng_ref[...]).reshape(m, CONV_W).astype(BF16)
    last2 = ubuf[:, hdr + ts - 2:hdr + ts, :]
    ubuf[:, hdr - 2:hdr, :] = last2

    r_i = lax.broadcasted_iota(I32, (CHUNK, CHUNK), 0)
    c_i = lax.broadcasted_iota(I32, (CHUNK, CHUNK), 1)
    causal = r_i >= c_i
    ltri = causal.astype(F32)
    utri = (r_i <= c_i).astype(F32)

    def chunk(b, c):
        t0 = pl.multiple_of(c * CHUNK, CHUNK)
        rows = pl.ds(t0, CHUNK)
        out_rows = pl.ds(pl.multiple_of(b * ts + t0, CHUNK), CHUNK)
        gcol = gc_ref[b, rows, :]
        grow = gr_ref[b, c]
        f_col = jnp.dot(ltri, gcol, precision=HIGHEST, preferred_element_type=F32)
        f_row = jnp.dot(grow, utri, precision=HIGHEST, preferred_element_type=F32)
        for h in range(ML_H):
            q = pm_ref[b, rows, h * ML_DK:(h + 1) * ML_DK]
            k = pm_ref[b, rows, ML_W + h * ML_DK:ML_W + (h + 1) * ML_DK]
            v = pm_ref[b, rows, 2 * ML_W + h * ML_DK:2 * ML_W + (h + 1) * ML_DK]
            og = pm_ref[b, rows, 3 * ML_W + h * ML_DK:3 * ML_W + (h + 1) * ML_DK]
            fc = f_col[:, ML_H + h:ML_H + h + 1]
            fr = f_row[ML_H + h:ML_H + h + 1, :]
            igc = gcol[:, h:h + 1]
            igr = grow[h:h + 1, :]
            dmat = jnp.where(causal, fc - fr + igr, -jnp.inf)
            m_prev = m_s[b, h:h + 1, 0:1]
            prior = fc + m_prev
            m_t = jnp.maximum(prior, jnp.max(dmat, axis=1, keepdims=True))
            w_prior = jnp.exp(prior - m_t)
            p = jnp.exp(dmat - m_t)
            qk = lax.dot_general(q, k, NT_DIMS, preferred_element_type=F32)
            s = qk * (p * K_SCALE)
            c_old = c_s[b, h]
            n_old = n_s[b, h:h + 1, :]
            num = (jnp.dot(s.astype(BF16), v, preferred_element_type=F32)
                   + w_prior * jnp.dot(q, c_old.astype(BF16), preferred_element_type=F32))
            den = (jnp.sum(s, axis=1, keepdims=True)
                   + w_prior * jnp.sum(q.astype(F32) * n_old, axis=1, keepdims=True))
            hh = num / jnp.maximum(jnp.abs(den), jnp.exp(-m_t))
            m_new = m_t[CHUNK - 1:CHUNK, :]
            f_last = fc[CHUNK - 1:CHUNK, :]
            a = jnp.exp(f_last + m_prev - m_new)
            ws = jnp.exp(f_last - fc + igc - m_new) * K_SCALE
            kw = k.astype(F32) * ws
            c_s[b, h] = a * c_old + jnp.dot(kw.T.astype(BF16), v, preferred_element_type=F32)
            n_s[b, h:h + 1, :] = a * n_old + jnp.sum(kw, axis=0, keepdims=True)
            m_s[b, h:h + 1, :] = jnp.broadcast_to(m_new, (1, LANES))
            hn = _rms(hh) * mng_ref[:, h * ML_DK:(h + 1) * ML_DK]
            hn = hn * jax.nn.sigmoid(og.astype(F32))
            hm_s[out_rows, h * ML_DK:(h + 1) * ML_DK] = hn

    def unit(u, carry):
        if nch == 1:
            g, c = u, 0
        elif nb == SEQ_PAR:
            g, c = 0, u
        else:
            g, c = u // nch, u % nch
        for j in range(SEQ_PAR):
            chunk(g * SEQ_PAR + j, c)
        return carry

    lax.fori_loop(0, (nb // SEQ_PAR) * nch, unit, 0)

    mix = (jnp.dot(co, wout_ref[0:CONV_W, :], preferred_element_type=F32)
           + jnp.dot(hm_s[...].astype(BF16), wout_ref[CONV_W:CONV_W + ML_W, :], preferred_element_type=F32))
    g1 = mod_ref[:, :, 2 * d:3 * d]
    sh2 = mod_ref[:, :, 3 * d:4 * d]
    sc2 = mod_ref[:, :, 4 * d:5 * d]
    xn = x_ref[...] + g1 * mix.reshape(nb, ts, d)
    xo_ref[...] = xn
    h2 = _rms(xn) * n2g_ref[...] * (1.0 + sc2) + sh2
    h2_ref[...] = _pack_pairs(h2)
    logits_t = _dot3(rwt_ref[...], h2.reshape(m, d), NT_DIMS)
    for b in range(nb):
        lt_ref[b] = logits_t[:, b * ts:(b + 1) * ts]

    @pl.when(si == pl.num_programs(1) - 1)
    def _store_state():
        convo_ref[...] = last2
        co_ref[...] = c_s[...]
        no_ref[...] = n_s[...]
        mo_ref[...] = m_s[...]


def _mixer(x, mod, pc, pm, gcol, grow, conv0, c0, n0, m0, conv_w, cn_g, mn_g, w_out, n2_g, rw_t):
    batch, seq, d = x.shape
    nb, ts = _seq_tile(batch, seq, ROW_TILE // SEQ_PAR)
    nb = min(batch, nb * SEQ_PAR)
    assert nb % SEQ_PAR == 0 and batch % nb == 0
    nch = ts // CHUNK
    n_si = seq // ts
    grid = (batch // nb, n_si)
    tile = lambda w: pl.BlockSpec((nb, ts, w), lambda bi, si: (bi, si, 0))
    full = lambda a: pl.BlockSpec(a.shape, lambda bi, si: (0,) * a.ndim)
    per_b = lambda *tail: pl.BlockSpec((nb,) + tail, lambda bi, si: (bi,) + (0,) * len(tail))
    return pl.pallas_call(
        _mixer_kernel,
        grid=grid,
        in_specs=[tile(d), per_b(1, 6 * d), tile(3 * CONV_W), tile(4 * ML_W), tile(LANES),
                  pl.BlockSpec((nb, nch, N_GATE, CHUNK), lambda bi, si: (bi, si, 0, 0)),
                  per_b(CONV_K - 1, CONV_W), per_b(ML_H, ML_DK, ML_DK), per_b(8, LANES), per_b(8, LANES),
                  full(conv_w), full(cn_g), full(mn_g), full(w_out), full(n2_g), full(rw_t)],
        out_specs=[tile(d), tile(d // 2),
                   pl.BlockSpec((nb, N_EXPERTS, ts), lambda bi, si: (bi, 0, si)),
                   per_b(CONV_K - 1, CONV_W), per_b(ML_H, ML_DK, ML_DK), per_b(8, LANES), per_b(8, LANES)],
        out_shape=[jax.ShapeDtypeStruct((batch, seq, d), F32),
                   jax.ShapeDtypeStruct((batch, seq, d // 2), U32),
                   jax.ShapeDtypeStruct((batch, N_EXPERTS, seq), F32),
                   jax.ShapeDtypeStruct((batch, CONV_K - 1, CONV_W), F32),
                   jax.ShapeDtypeStruct((batch, ML_H, ML_DK, ML_DK), F32),
                   jax.ShapeDtypeStruct((batch, 8, LANES), F32),
                   jax.ShapeDtypeStruct((batch, 8, LANES), F32)],
        scratch_shapes=[pltpu.VMEM((nb, ts + 8, CONV_W), F32),
                        pltpu.VMEM((nb, ML_H, ML_DK, ML_DK), F32),
                        pltpu.VMEM((nb, 8, LANES), F32),
                        pltpu.VMEM((nb, 8, LANES), F32),
                        pltpu.VMEM((nb * ts, ML_W), F32)],
        compiler_params=_cparams("parallel", "arbitrary"),
        name="mixer",
    )(x, mod, pc, pm, gcol, grow, conv0, c0, n0, m0, conv_w, cn_g, mn_g, w_out, n2_g, rw_t)


def _route_kernel(lt_ref, rb_ref, tri_ref, wts_ref, lslot_ref, tcnt_ref, tbase_ref, cnt_ref, cnt_s):
    tt = lt_ref.shape[1]

    @pl.when(pl.program_id(0) == 0)
    def _zero():
        cnt_s[...] = jnp.zeros_like(cnt_s)

    neg = -jnp.inf
    scores = jax.nn.sigmoid(lt_ref[...])
    sel3 = (scores + rb_ref[...]).reshape(N_GROUPS, GROUP_SIZE, tt)
    sc3 = scores.reshape(N_GROUPS, GROUP_SIZE, tt)
    shape3 = (N_GROUPS, GROUP_SIZE, tt)
    j3 = lax.broadcasted_iota(I32, shape3, 1).astype(F32)
    g3 = lax.broadcasted_iota(I32, shape3, 0).astype(F32)
    e3 = g3 * GROUP_SIZE + j3
    m1 = jnp.max(sel3, axis=1, keepdims=True)
    first = jnp.min(jnp.where(sel3 == m1, j3, float(GROUP_SIZE)), axis=1, keepdims=True)
    m2 = jnp.max(jnp.where(j3 == first, neg, sel3), axis=1, keepdims=True)
    grp = m1 + m2
    gi = lax.broadcasted_iota(I32, grp.shape, 0).astype(F32)
    gmask = jnp.zeros(grp.shape, F32)
    for _ in range(TOPK_GROUPS):
        gmax = jnp.max(grp, axis=0, keepdims=True)
        gidx = jnp.min(jnp.where(grp == gmax, gi, float(N_GROUPS)), axis=0, keepdims=True)
        hit = gi == gidx
        gmask = jnp.where(hit, 1.0, gmask)
        grp = jnp.where(hit, neg, grp)
    cand = jnp.where(gmask > 0.0, sel3, neg)

    def red(fn, x):
        return fn(fn(x, axis=1, keepdims=True), axis=0, keepdims=True)

    selm = jnp.zeros(shape3, F32)
    idxs, ws = [], []
    for _ in range(TOP_K):
        cmax = red(jnp.max, cand)
        eidx = red(jnp.min, jnp.where(cand == cmax, e3, float(N_EXPERTS)))
        hit = e3 == eidx
        idxs.append(eidx)
        ws.append(red(jnp.sum, jnp.where(hit, sc3, 0.0)))
        selm = jnp.where(hit, 1.0, selm)
        cand = jnp.where(hit, neg, cand)
    wsum = ws[0]
    for k in range(1, TOP_K):
        wsum = wsum + ws[k]
    sel2 = selm.reshape(N_EXPERTS, tt)
    lcum = jnp.dot(sel2.astype(BF16), tri_ref[...], preferred_element_type=F32)
    base = cnt_s[...]
    tcnt = jnp.ceil(jnp.sum(sel2, axis=1, keepdims=True) * (1.0 / ROW_GROUP)) * ROW_GROUP
    tcnt = jnp.broadcast_to(tcnt, (N_EXPERTS, LANES))
    r_e = lax.broadcasted_iota(I32, (N_EXPERTS, N_EXPERTS), 0)
    c_e = lax.broadcasted_iota(I32, (N_EXPERTS, N_EXPERTS), 1)
    loff = jnp.dot((c_e < r_e).astype(BF16), tcnt.astype(BF16), preferred_element_type=F32)
    lsl3 = (lcum + loff[:, 0:1]).reshape(shape3)
    for k in range(TOP_K):
        hit = e3 == idxs[k]
        wts_ref[k:k + 1, :] = (ws[k] / wsum * ROUTED_SCALE).reshape(1, tt)
        lslot_ref[k:k + 1, :] = red(jnp.sum, jnp.where(hit, lsl3, 0.0)).reshape(1, tt).astype(I32)
    tcnt_ref[0] = tcnt
    tbase_ref[0] = base
    cnt_s[...] = base + tcnt
    cnt_ref[...] = cnt_s[...]


def _route(logits_t, router_b):
    tokens = logits_t.shape[1]
    tt = MOE_TILE
    n_tiles = tokens // tt
    r_i = lax.broadcasted_iota(I32, (tt, tt), 0)
    c_i = lax.broadcasted_iota(I32, (tt, tt), 1)
    tri = (r_i < c_i).astype(BF16)
    row = lambda: pl.BlockSpec((TOP_K, tt), lambda i: (0, i))
    stat = lambda: pl.BlockSpec((1, N_EXPERTS, LANES), lambda i: (i, 0, 0))
    return pl.pallas_call(
        _route_kernel,
        grid=(n_tiles,),
        in_specs=[pl.BlockSpec((N_EXPERTS, tt), lambda i: (0, i)),
                  pl.BlockSpec((N_EXPERTS, 1), lambda i: (0, 0)),
                  pl.BlockSpec((tt, tt), lambda i: (0, 0))],
        out_specs=[row(), row(), stat(), stat(),
                   pl.BlockSpec((N_EXPERTS, LANES), lambda i: (0, 0))],
        out_shape=[jax.ShapeDtypeStruct((TOP_K, tokens), F32),
                   jax.ShapeDtypeStruct((TOP_K, tokens), I32),
                   jax.ShapeDtypeStruct((n_tiles, N_EXPERTS, LANES), F32),
                   jax.ShapeDtypeStruct((n_tiles, N_EXPERTS, LANES), F32),
                   jax.ShapeDtypeStruct((N_EXPERTS, LANES), F32)],
        scratch_shapes=[pltpu.VMEM((N_EXPERTS, LANES), F32)],
        compiler_params=_cparams("arbitrary"),
        name="route",
    )(logits_t, router_b.reshape(N_EXPERTS, 1), tri)


MOE_TILE = 256
ROW_GROUP = 8
SLAB = 32
SORT_CHUNK = 512
FILL_ROWS = 64


def _dispatch_kernel(fill_ref, tab_ref, lslot_ref, h_ref, xs_hbm, xbuf, zbuf, pending, sem, fill_sem):
    tt, dw = h_ref.shape
    i = pl.program_id(0)
    slot = lax.rem(i, 2)
    n_chunks = xbuf.shape[1] // SORT_CHUNK

    @pl.when(i == 0)
    def _zero_fill():
        zbuf[...] = jnp.zeros_like(zbuf)

        def zero_piece(rows, dst):
            return pltpu.make_async_copy(zbuf.at[pl.ds(0, rows)], xs_hbm.at[pl.ds(dst, rows)], fill_sem)

        def run(n, rows, start, do_start):
            def body(s, carry):
                cp = zero_piece(rows, start + pl.multiple_of(s * rows, rows))
                if do_start:
                    cp.start()
                else:
                    cp.wait()
                return carry
            lax.fori_loop(0, n, body, 0)

        def per_expert(e, carry):
            start = pl.multiple_of(fill_ref[0, 0, e], ROW_GROUP)
            n_big = fill_ref[0, 0, N_EXPERTS + e]
            run(n_big, SLAB, start, True)
            run(fill_ref[0, 0, 2 * N_EXPERTS + e], ROW_GROUP,
                start + pl.multiple_of(n_big * SLAB, SLAB), True)
            return carry

        lax.fori_loop(0, N_EXPERTS, per_expert, 0)
        n_tail = fill_ref[0, 0, 3 * N_EXPERTS + 1]
        run(n_tail, FILL_ROWS, pl.multiple_of(fill_ref[0, 0, 3 * N_EXPERTS], FILL_ROWS), True)
        run(fill_ref[0, 0, 3 * N_EXPERTS + 2], SLAB, 0, False)
        run(fill_ref[0, 0, 3 * N_EXPERTS + 3], ROW_GROUP, 0, False)
        run(n_tail, FILL_ROWS, 0, False)

    lo, hi = _unpack_pairs(h_ref[...])
    lo = lo.astype(BF16)
    hi = hi.astype(BF16)
    lslot = lslot_ref[...]
    for c in range(n_chunks):
        r_i = lax.broadcasted_iota(I32, (SORT_CHUNK, tt), 0) + c * SORT_CHUNK
        perm = jnp.zeros((SORT_CHUNK, tt), F32)
        for k in range(TOP_K):
            perm = jnp.where(r_i == lslot[k:k + 1, :], 1.0, perm)
        perm = perm.astype(BF16)
        x_lo = jnp.dot(perm, lo, preferred_element_type=F32)
        x_hi = jnp.dot(perm, hi, preferred_element_type=F32)
        xbuf[slot, c * SORT_CHUNK:(c + 1) * SORT_CHUNK, :] = _pack_pairs(
            jnp.concatenate([x_lo, x_hi], axis=-1))

    def piece(rows, sl, src, dst):
        return pltpu.make_async_copy(xbuf.at[sl, pl.ds(src, rows)], xs_hbm.at[pl.ds(dst, rows)], sem)

    def drain(n_big, n_small):
        def big(_, carry):
            piece(SLAB, 0, 0, 0).wait()
            return carry

        def small(_, carry):
            piece(ROW_GROUP, 0, 0, 0).wait()
            return carry

        lax.fori_loop(0, n_big, big, 0)
        lax.fori_loop(0, n_small, small, 0)

    @pl.when(i > 0)
    def _wait_previous():
        drain(pending[0], pending[1])

    def per_expert(e, carry):
        dst = pl.multiple_of(tab_ref[0, 0, e], ROW_GROUP)
        src = pl.multiple_of(tab_ref[0, 0, N_EXPERTS + e], ROW_GROUP)
        n_big = tab_ref[0, 0, 2 * N_EXPERTS + e]
        n_small = tab_ref[0, 0, 3 * N_EXPERTS + e]

        def big(s, c2):
            off = pl.multiple_of(s * SLAB, SLAB)
            piece(SLAB, slot, src + off, dst + off).start()
            return c2

        def small(s, c2):
            off = pl.multiple_of(n_big * SLAB + s * ROW_GROUP, ROW_GROUP)
            piece(ROW_GROUP, slot, src + off, dst + off).start()
            return c2

        lax.fori_loop(0, n_big, big, 0)
        lax.fori_loop(0, n_small, small, 0)
        return carry

    lax.fori_loop(0, N_EXPERTS, per_expert, 0)
    pending[0] = tab_ref[0, 0, 4 * N_EXPERTS]
    pending[1] = tab_ref[0, 0, 4 * N_EXPERTS + 1]

    @pl.when(i == pl.num_programs(0) - 1)
    def _wait_last():
        drain(pending[0], pending[1])


def _dispatch(fill, tab, lslot, h2p_flat, rows_total):
    tokens, dw = h2p_flat.shape
    tt = MOE_TILE
    local_rows = pl.cdiv(TOP_K * tt + N_EXPERTS * (ROW_GROUP - 1), SORT_CHUNK) * SORT_CHUNK
    return pl.pallas_call(
        _dispatch_kernel,
        grid=(tokens // tt,),
        in_specs=[pl.BlockSpec(fill.shape, lambda i: (0, 0, 0), memory_space=pltpu.SMEM),
                  pl.BlockSpec((1, 1, tab.shape[2]), lambda i: (i, 0, 0), memory_space=pltpu.SMEM),
                  pl.BlockSpec((TOP_K, tt), lambda i: (0, i)),
                  pl.BlockSpec((tt, dw), lambda i: (i, 0))],
        out_specs=pl.BlockSpec(memory_space=pl.ANY),
        out_shape=jax.ShapeDtypeStruct((rows_total, dw), h2p_flat.dtype),
        scratch_shapes=[pltpu.VMEM((2, local_rows, dw), U32), pltpu.VMEM((FILL_ROWS, dw), U32),
                        pltpu.SMEM((2,), I32), pltpu.SemaphoreType.DMA, pltpu.SemaphoreType.DMA],
        compiler_params=_cparams("arbitrary"),
        name="dispatch",
    )(fill, tab, lslot, h2p_flat)


def _swiglu_packed(xw, w1, w3, w2):
    lo, hi = _unpack_pairs(xw)
    x = jnp.concatenate([lo.astype(BF16), hi.astype(BF16)], axis=-1)
    a = jnp.dot(x, w1, preferred_element_type=F32)
    b = jnp.dot(x, w3, preferred_element_type=F32)
    hmid = (a * jax.nn.sigmoid(a) * b).astype(BF16)
    return jnp.dot(hmid, w2, preferred_element_type=F32)


def _expert_kernel(be_ref, nu_ref, x_ref, w1_ref, w3_ref, w2_ref, y_ref):
    del be_ref
    j = pl.program_id(0)

    @pl.when(j < nu_ref[0])
    def _compute():
        y_ref[...] = _pack_pairs(_swiglu_packed(x_ref[...], w1_ref[0], w3_ref[0], w2_ref[0]))

    @pl.when(j >= nu_ref[0])
    def _unused():
        y_ref[...] = jnp.zeros_like(y_ref)


def _experts(block_e, n_used, xs, w1, w3, w2, rb):
    rows, dw = xs.shape
    nblk = rows // rb
    d, f = w1.shape[1:]
    return pl.pallas_call(
        _expert_kernel,
        grid_spec=pltpu.PrefetchScalarGridSpec(
            num_scalar_prefetch=2,
            grid=(nblk,),
            in_specs=[pl.BlockSpec((rb, dw), lambda j, be, nu: (jnp.minimum(j, nu[0] - 1), 0)),
                      pl.BlockSpec((1, d, f), lambda j, be, nu: (be[j], 0, 0)),
                      pl.BlockSpec((1, d, f), lambda j, be, nu: (be[j], 0, 0)),
                      pl.BlockSpec((1, f, d), lambda j, be, nu: (be[j], 0, 0))],
            out_specs=pl.BlockSpec((rb, dw), lambda j, be, nu: (j, 0)),
        ),
        out_shape=jax.ShapeDtypeStruct((rows, dw), U32),
        compiler_params=_cparams("arbitrary"),
        name="experts",
    )(block_e, n_used, xs, w1, w3, w2)


def _combine_kernel(tab0_ref, tabc_ref, tabn_ref, y_hbm, lslot_ref, wts_ref, h2_ref, x_ref, mod_ref,
                    sw1_ref, sw3_ref, sw2_ref, fg_ref, out_ref, ybuf, sem, *, final):
    nb, ts, d = x_ref.shape
    tc = nb * ts
    dw = d // 2
    i = pl.program_id(0)
    slot = lax.rem(i, 2)
    other = 1 - slot
    n_chunks = ybuf.shape[1] // SORT_CHUNK

    def piece(rows, sl, src, dst):
        return pltpu.make_async_copy(y_hbm.at[pl.ds(src, rows)], ybuf.at[sl, pl.ds(dst, rows)], sem.at[sl])

    def fetch(tab, sl):
        def per_expert(e, carry):
            src = pl.multiple_of(tab[0, 0, e], ROW_GROUP)
            dst = pl.multiple_of(tab[0, 0, N_EXPERTS + e], ROW_GROUP)
            n_big = tab[0, 0, 2 * N_EXPERTS + e]
            n_small = tab[0, 0, 3 * N_EXPERTS + e]

            def big(s, c2):
                off = pl.multiple_of(s * SLAB, SLAB)
                piece(SLAB, sl, src + off, dst + off).start()
                return c2

            def small(s, c2):
                off = pl.multiple_of(n_big * SLAB + s * ROW_GROUP, ROW_GROUP)
                piece(ROW_GROUP, sl, src + off, dst + off).start()
                return c2

            lax.fori_loop(0, n_big, big, 0)
            lax.fori_loop(0, n_small, small, 0)
            return carry

        lax.fori_loop(0, N_EXPERTS, per_expert, 0)

    def drain(tab, sl):
        def big(_, carry):
            piece(SLAB, sl, 0, 0).wait()
            return carry

        def small(_, carry):
            piece(ROW_GROUP, sl, 0, 0).wait()
            return carry

        lax.fori_loop(0, tab[0, 0, 4 * N_EXPERTS], big, 0)
        lax.fori_loop(0, tab[0, 0, 4 * N_EXPERTS + 1], small, 0)

    @pl.when(i == 0)
    def _first_tile():
        ybuf[...] = jnp.zeros_like(ybuf)
        fetch(tab0_ref, 0)

    fetch(tabn_ref, other)
    shared = _swiglu_packed(h2_ref[...].reshape(tc, dw), sw1_ref[...], sw3_ref[...], sw2_ref[...])
    drain(tabc_ref, slot)

    lslot = lslot_ref[...]
    wts = wts_ref[...]
    acc_lo = shared[:, :dw]
    acc_hi = shared[:, dw:]
    for c in range(n_chunks):
        r_i = lax.broadcasted_iota(I32, (SORT_CHUNK, tc), 0) + c * SORT_CHUNK
        wt = jnp.zeros((SORT_CHUNK, tc), F32)
        for k in range(TOP_K):
            wt = jnp.where(r_i == lslot[k:k + 1, :], wts[k:k + 1, :], wt)
        wm = wt.T.astype(BF16)
        lo, hi = _unpack_pairs(ybuf[slot, c * SORT_CHUNK:(c + 1) * SORT_CHUNK, :])
        acc_lo = acc_lo + jnp.dot(wm, lo.astype(BF16), preferred_element_type=F32)
        acc_hi = acc_hi + jnp.dot(wm, hi.astype(BF16), preferred_element_type=F32)
    acc = jnp.concatenate([acc_lo, acc_hi], axis=-1)
    xn = x_ref[...] + mod_ref[:, :, 5 * d:6 * d] * acc.reshape(nb, ts, d)
    if final:
        xn = _rms(xn) * fg_ref[...]
    out_ref[...] = xn

    @pl.when(i == pl.num_programs(0) - 1)
    def _last_tile():
        drain(tabn_ref, other)


def _combine(tab, y_sorted, lslot_t, wts_t, h2p, x, mod, sw1, sw3, sw2, final_g, final):
    batch, seq, d = x.shape
    dw = d // 2
    nb, ts = _seq_tile(batch, seq, MOE_TILE)
    tc = nb * ts
    n_si = seq // ts
    n_tiles = (batch // nb) * n_si
    local_rows = pl.cdiv(TOP_K * tc + N_EXPERTS * (ROW_GROUP - 1), SORT_CHUNK) * SORT_CHUNK
    tile = lambda w: pl.BlockSpec((nb, ts, w), lambda i: (i // n_si, i % n_si, 0))
    full = lambda a: pl.BlockSpec(a.shape, lambda i: (0,) * a.ndim)
    tab_spec = lambda f: pl.BlockSpec((1, 1, tab.shape[2]), lambda i: (f(i), 0, 0), memory_space=pltpu.SMEM)
    per_token = lambda: pl.BlockSpec((TOP_K, tc), lambda i: (0, i))
    return pl.pallas_call(
        functools.partial(_combine_kernel, final=final),
        grid=(n_tiles,),
        in_specs=[tab_spec(lambda i: 0), tab_spec(lambda i: i),
                  tab_spec(lambda i: jnp.minimum(i + 1, n_tiles - 1)),
                  pl.BlockSpec(memory_space=pl.ANY), per_token(), per_token(),
                  tile(dw), tile(d), pl.BlockSpec((nb, 1, 6 * d), lambda i: (i // n_si, 0, 0)),
                  full(sw1), full(sw3), full(sw2), full(final_g)],
        out_specs=tile(d),
        out_shape=jax.ShapeDtypeStruct((batch, seq, d), F32),
        scratch_shapes=[pltpu.VMEM((2, local_rows, dw), U32), pltpu.SemaphoreType.DMA((2,))],
        compiler_params=_cparams("arbitrary"),
        name="combine",
    )(tab, tab, tab, y_sorted, lslot_t, wts_t, h2p, x, mod, sw1, sw3, sw2, final_g)


def _expert_block_rows(tokens):
    if tokens >= 16384:
        return 1024
    if tokens >= 2048:
        return 256
    return 64


def _routing_tables(counts, tcnt, tbase, rb, nblk):
    counts = counts.astype(I32)
    padded = (counts + rb - 1) // rb * rb
    pends = jnp.cumsum(padded)
    pstarts = pends - padded
    block_start = jnp.arange(nblk, dtype=I32) * rb
    block_e = jnp.minimum(jnp.sum(pends[None, :] <= block_start[:, None], axis=1), N_EXPERTS - 1).astype(I32)
    n_used = (pends[-1:] // rb).astype(I32)
    tcnt = tcnt.astype(I32)
    first_row = pstarts[None, :] + tbase.astype(I32)
    local_off = jnp.cumsum(tcnt, axis=1) - tcnt
    n_big = tcnt // SLAB
    n_small = (tcnt - n_big * SLAB) // ROW_GROUP
    totals = jnp.stack([jnp.sum(n_big, axis=1), jnp.sum(n_small, axis=1)], axis=1)
    tab = jnp.concatenate([first_row, local_off, n_big, n_small, totals], axis=1)
    tab = jnp.pad(tab, ((0, 0), (0, 5 * N_EXPERTS - tab.shape[1])))[:, None, :]
    pad_rows = padded - counts
    pad_big = pad_rows // SLAB
    pad_small = (pad_rows - pad_big * SLAB) // ROW_GROUP
    tail = jnp.stack([pends[-1], (nblk * rb - pends[-1]) // FILL_ROWS, jnp.sum(pad_big), jnp.sum(pad_small)])
    fill = jnp.concatenate([pstarts + counts, pad_big, pad_small, tail])
    fill = jnp.pad(fill, (0, 4 * N_EXPERTS - fill.shape[0]))[None, None, :]
    return block_e, n_used, tab, fill


def _layer(x, mod, conv0, c0, n0, m0, p, final_g, final):
    batch, seq, d = x.shape
    tokens = batch * seq
    pc, pm, gcol, grow = _inproj(x, mod, p["norm1_g"], p["w_main"], p["wg_col"], p["b_col"])
    grow = grow.reshape(batch, N_GATE, seq // CHUNK, CHUNK).transpose(0, 2, 1, 3)
    m0b = jnp.broadcast_to(jnp.pad(m0, ((0, 0), (0, 8 - ML_H)))[:, :, None], (batch, 8, LANES))
    n0b = jnp.pad(n0, ((0, 0), (0, 8 - ML_H), (0, 0)))
    x1, h2, logits_t, conv_new, c_new, n_new, m_new = _mixer(
        x, mod, pc, pm, gcol, grow, conv0, c0, n0b, m0b, p["conv_w"], p["conv_norm_g"],
        p["mlstm_norm_g"], p["w_out"], p["norm2_g"], p["router_wt"])
    logits_t = logits_t.transpose(1, 0, 2).reshape(N_EXPERTS, tokens)
    wts, lslot, tcnt, tbase, cnt = _route(logits_t, p["router_b"])
    rb = _expert_block_rows(tokens)
    n_tiles = tokens // MOE_TILE
    nblk = pl.cdiv(tokens * TOP_K + n_tiles * N_EXPERTS * (ROW_GROUP - 1), rb) + N_EXPERTS
    block_e, n_used, tab, fill = _routing_tables(cnt[:, 0], tcnt[:, :, 0], tbase[:, :, 0], rb, nblk)
    xs = _dispatch(fill, tab, lslot, h2.reshape(tokens, d // 2), nblk * rb)
    ys = _experts(block_e, n_used, xs, p["exp_w1"], p["exp_w3"], p["exp_w2"], rb)
    out = _combine(tab, ys, lslot, wts, h2, x1, mod, p["shared_w1"], p["shared_w3"], p["shared_w2"],
                   final_g, final)
    return out, conv_new, c_new, n_new[:, :ML_H, :], m_new[:, :ML_H, 0]


def _trunk(x, mods, conv0, c0, n0, m0, layers, final_g):
    depth = len(layers)
    convs, cs, ns, ms = [], [], [], []
    for l in range(depth):
        x, cv, c, n, m = _layer(x, mods[l], conv0[l], c0[l], n0[l], m0[l], layers[l], final_g,
                                final=(l == depth - 1))
        convs.append(cv); cs.append(c); ns.append(n); ms.append(m)
    return x, jnp.stack(convs), jnp.stack(cs), jnp.stack(ns), jnp.stack(ms)


def kernel(x_prompt, x_sample, c_prompt, c_sample, cache_conv, state_mlstm_C, state_mlstm_n, state_mlstm_m,
           norm1_g, ada_w, ada_b, w_in, b_igate, b_fgate, conv_w, conv_norm_g, mlstm_norm_g, w_out, norm2_g,
           router_w, router_b, exp_w1, exp_w3, exp_w2, shared_w1, shared_w3, shared_w2, final_g):
    depth = w_in.shape[0]
    bp = x_prompt.shape[0]
    d = x_prompt.shape[-1]
    mods = _ada(jnp.concatenate([c_prompt, c_sample], axis=0), ada_w, ada_b)
    mods_p = mods[:, :bp, None, :]
    mods_s = mods[:, bp:, None, :]

    w_gate = w_in[:, :, N_MAIN:]
    bias = jnp.concatenate([b_igate, b_fgate], axis=-1)
    layers = []
    for l in range(depth):
        layers.append(dict(
            norm1_g=norm1_g[l][None, :],
            w_main=w_in[l, :, :N_MAIN].astype(BF16),
            wg_col=jnp.pad(w_gate[l], ((0, 0), (0, LANES - N_GATE))),
            b_col=jnp.pad(bias[l], (0, LANES - N_GATE))[None, :],
            conv_w=conv_w[l], conv_norm_g=conv_norm_g[l][None, :], mlstm_norm_g=mlstm_norm_g[l][None, :],
            w_out=w_out[l].astype(BF16), norm2_g=norm2_g[l][None, :],
            router_wt=router_w[l].T, router_b=router_b[l],
            exp_w1=exp_w1[l].astype(BF16), exp_w3=exp_w3[l].astype(BF16), exp_w2=exp_w2[l].astype(BF16),
            shared_w1=shared_w1[l].astype(BF16), shared_w3=shared_w3[l].astype(BF16),
            shared_w2=shared_w2[l].astype(BF16)))
    fg = final_g[None, :]

    zeros = lambda *s: jnp.zeros((depth, bp) + s, x_prompt.dtype)
    y_p, conv_p, c_p, n_p, m_p = _trunk(
        x_prompt, mods_p, zeros(CONV_K - 1, CONV_W), zeros(ML_H, ML_DK, ML_DK), zeros(ML_H, ML_DK),
        zeros(ML_H), layers, fg)
    y_s, conv_s, c_s, n_s, m_s = _trunk(
        x_sample, mods_s, cache_conv, state_mlstm_C, state_mlstm_n, state_mlstm_m, layers, fg)
    return (y_p, y_s, conv_p, c_p, n_p, m_p, conv_s, c_s, n_s, m_s)
```

```python
import functools

import jax
import jax.numpy as jnp
from jax import lax
from jax.experimental import pallas as pl
from jax.experimental.pallas import tpu as pltpu

F32 = jnp.float32
BF16 = jnp.bfloat16
I32 = jnp.int32
U32 = jnp.uint32
HIGHEST = lax.Precision.HIGHEST

D_MODEL = 1024
CONV_W = 512
CONV_K = 3
ML_W = 512
ML_H = 4
ML_DK = 128
CHUNK = 64
N_EXPERTS = 64
TOP_K = 8
N_GROUPS = 8
GROUP_SIZE = N_EXPERTS // N_GROUPS
TOPK_GROUPS = 4
D_EXPERT = 256
ROUTED_SCALE = 2.5
EPS = 1e-6
K_SCALE = ML_DK ** -0.5
N_MAIN = 3 * CONV_W + 4 * ML_W
N_GATE = 2 * ML_H
LANES = 128
ROW_TILE = 512
VMEM_LIMIT = 56 * 1024 * 1024

NT_DIMS = (((1,), (1,)), ((), ()))


def _cparams(*sem):
    return pltpu.CompilerParams(dimension_semantics=sem, vmem_limit_bytes=VMEM_LIMIT)


def _rms(x):
    return x * lax.rsqrt(jnp.mean(x * x, axis=-1, keepdims=True) + EPS)


def _dot3(a, b, dims=None):
    def split(x):
        hi = x.astype(BF16)
        return hi, (x - hi.astype(F32)).astype(BF16)

    def dot(x, y):
        if dims is None:
            return jnp.dot(x, y, preferred_element_type=F32)
        return lax.dot_general(x, y, dims, preferred_element_type=F32)

    a_hi, a_lo = split(a)
    b_hi, b_lo = split(b)
    return dot(a_hi, b_hi) + dot(a_hi, b_lo) + dot(a_lo, b_hi)


def _log_sigmoid(x):
    return jnp.minimum(x, 0.0) - jnp.log1p(jnp.exp(-jnp.abs(x)))


def _pack_pairs(x):
    n = x.shape[-1] // 2
    packed = pltpu.pack_elementwise([x[..., :n], x[..., n:]], packed_dtype=BF16)
    return lax.bitcast_convert_type(packed, U32)


def _unpack_pairs(w):
    w = lax.bitcast_convert_type(w, I32)
    lo = pltpu.unpack_elementwise(w, index=0, packed_dtype=BF16, unpacked_dtype=F32)
    hi = pltpu.unpack_elementwise(w, index=1, packed_dtype=BF16, unpacked_dtype=F32)
    return lo, hi


def _seq_tile(batch, seq, rows):
    if seq >= rows:
        return 1, rows
    return min(batch, rows // seq), seq


def _ada_kernel(c_ref, w_ref, b_ref, o_ref):
    c = c_ref[...]
    s = (c * jax.nn.sigmoid(c)).astype(BF16)
    o_ref[0] = jnp.dot(s, w_ref[0].astype(BF16), preferred_element_type=F32) + b_ref[0]


def _ada(c_all, ada_w, ada_b):
    depth, d, n = ada_w.shape
    nb = c_all.shape[0]
    tn = 1536
    return pl.pallas_call(
        _ada_kernel,
        grid=(depth, n // tn),
        in_specs=[
            pl.BlockSpec((nb, d), lambda l, j: (0, 0)),
            pl.BlockSpec((1, d, tn), lambda l, j: (l, 0, j)),
            pl.BlockSpec((1, 1, tn), lambda l, j: (l, 0, j)),
        ],
        out_specs=pl.BlockSpec((1, nb, tn), lambda l, j: (l, 0, j)),
        out_shape=jax.ShapeDtypeStruct((depth, nb, n), F32),
        compiler_params=_cparams("parallel", "parallel"),
        name="ada",
    )(c_all, ada_w, ada_b.reshape(depth, 1, n))


def _inproj_kernel(x_ref, mod_ref, g_ref, wm_ref, wgc_ref, bc_ref, pc_ref, pm_ref, gc_ref, gr_ref):
    nb, ts, d = x_ref.shape
    m = nb * ts
    sh = mod_ref[:, :, 0:d]
    sc = mod_ref[:, :, d:2 * d]
    h = (_rms(x_ref[...]) * g_ref[...] * (1.0 + sc) + sh).reshape(m, d)
    hb = h.astype(BF16)
    cn = 512
    for j in range(N_MAIN // cn):
        pj = jnp.dot(hb, wm_ref[:, j * cn:(j + 1) * cn], preferred_element_type=F32).astype(BF16)
        if j < 3:
            pc_ref[:, :, j * cn:(j + 1) * cn] = pj.reshape(nb, ts, cn)
        else:
            pm_ref[:, :, (j - 3) * cn:(j - 2) * cn] = pj.reshape(nb, ts, cn)
    gcol = _dot3(h, wgc_ref[...]) + bc_ref[...]
    lane = lax.broadcasted_iota(I32, gcol.shape, 1)
    gcol = jnp.where(lane >= ML_H, _log_sigmoid(gcol), gcol)
    gc_ref[...] = gcol.reshape(nb, ts, LANES)
    grow = gcol.T[:N_GATE]
    for b in range(nb):
        gr_ref[b] = grow[:, b * ts:(b + 1) * ts]


def _inproj(x, mod, norm_g, w_main, wg_col, b_col):
    batch, seq, d = x.shape
    nb, ts = _seq_tile(batch, seq, ROW_TILE)
    grid = (batch // nb, seq // ts)
    tile = lambda w: pl.BlockSpec((nb, ts, w), lambda bi, si: (bi, si, 0))
    full = lambda a: pl.BlockSpec(a.shape, lambda bi, si: (0,) * a.ndim)
    return pl.pallas_call(
        _inproj_kernel,
        grid=grid,
        in_specs=[tile(d), pl.BlockSpec((nb, 1, 6 * d), lambda bi, si: (bi, 0, 0)),
                  full(norm_g), full(w_main), full(wg_col), full(b_col)],
        out_specs=[tile(3 * CONV_W), tile(4 * ML_W), tile(LANES),
                   pl.BlockSpec((nb, N_GATE, ts), lambda bi, si: (bi, 0, si))],
        out_shape=[jax.ShapeDtypeStruct((batch, seq, 3 * CONV_W), BF16),
                   jax.ShapeDtypeStruct((batch, seq, 4 * ML_W), BF16),
                   jax.ShapeDtypeStruct((batch, seq, LANES), F32),
                   jax.ShapeDtypeStruct((batch, N_GATE, seq), F32)],
        compiler_params=_cparams("parallel", "parallel"),
        name="inproj",
    )(x, mod, norm_g, w_main, wg_col, b_col)


SEQ_PAR = 4


def _mixer_kernel(x_ref, mod_ref, pc_ref, pm_ref, gc_ref, gr_ref, conv0_ref, c0_ref, n0_ref, m0_ref,
                  convw_ref, cng_ref, mng_ref, wout_ref, n2g_ref, rwt_ref,
                  xo_ref, h2_ref, lt_ref, convo_ref, co_ref, no_ref, mo_ref,
                  ubuf, c_s, n_s, m_s, hm_s):
    nb, ts, d = x_ref.shape
    nch = ts // CHUNK
    m = nb * ts
    si = pl.program_id(1)
    hdr = 8

    @pl.when(si == 0)
    def _load_state():
        ubuf[:, hdr - 2:hdr, :] = conv0_ref[...]
        c_s[...] = c0_ref[...]
        n_s[...] = n0_ref[...]
        m_s[...] = m0_ref[...]

    cb = pc_ref[:, :, 0:CONV_W].astype(F32)
    cc = pc_ref[:, :, CONV_W:2 * CONV_W].astype(F32)
    ch = pc_ref[:, :, 2 * CONV_W:3 * CONV_W].astype(F32)
    ubuf[:, hdr:hdr + ts, :] = cc * ch
    w = convw_ref[...]
    y = (w[0:1] * ubuf[:, hdr - 2:hdr - 2 + ts, :] + w[1:2] * ubuf[:, hdr - 1:hdr - 1 + ts, :]
         + w[2:3] * ubuf[:, hdr:hdr + ts, :])
    co = (_rms(cb * y) * cng_ref[...]).reshape(m, CONV_W).astype(BF16)
    last2 = ubuf[:, hdr + ts - 2:hdr + ts, :]
    ubuf[:, hdr - 2:hdr, :] = last2

    r_i = lax.broadcasted_iota(I32, (CHUNK, CHUNK), 0)
    c_i = lax.broadcasted_iota(I32, (CHUNK, CHUNK), 1)
    causal = r_i >= c_i
    ltri = causal.astype(F32)
    utri = (r_i <= c_i).astype(F32)

    def chunk(b, c):
        t0 = pl.multiple_of(c * CHUNK, CHUNK)
        rows = pl.ds(t0, CHUNK)
        out_rows = pl.ds(pl.multiple_of(b * ts + t0, CHUNK), CHUNK)
        gcol = gc_ref[b, rows, :]
        grow = gr_ref[b, c]
        f_col = jnp.dot(ltri, gcol, precision=HIGHEST, preferred_element_type=F32)
        f_row = jnp.dot(grow, utri, precision=HIGHEST, preferred_element_type=F32)
        for h in range(ML_H):
            q = pm_ref[b, rows, h * ML_DK:(h + 1) * ML_DK]
            k = pm_ref[b, rows, ML_W + h * ML_DK:ML_W + (h + 1) * ML_DK]
            v = pm_ref[b, rows, 2 * ML_W + h * ML_DK:2 * ML_W + (h + 1) * ML_DK]
            og = pm_ref[b, rows, 3 * ML_W + h * ML_DK:3 * ML_W + (h + 1) * ML_DK]
            fc = f_col[:, ML_H + h:ML_H + h + 1]
            fr = f_row[ML_H + h:ML_H + h + 1, :]
            igc = gcol[:, h:h + 1]
            igr = grow[h:h + 1, :]
            dmat = jnp.where(causal, fc - fr + igr, -jnp.inf)
            m_prev = m_s[b, h:h + 1, 0:1]
            prior = fc + m_prev
            m_t = jnp.maximum(prior, jnp.max(dmat, axis=1, keepdims=True))
            w_prior = jnp.exp(prior - m_t)
            p = jnp.exp(dmat - m_t)
            qk = lax.dot_general(q, k, NT_DIMS, preferred_element_type=F32)
            s = qk * (p * K_SCALE)
            c_old = c_s[b, h]
            n_old = n_s[b, h:h + 1, :]
            num = (jnp.dot(s.astype(BF16), v, preferred_element_type=F32)
                   + w_prior * jnp.dot(q, c_old.astype(BF16), preferred_element_type=F32))
            den = (jnp.sum(s, axis=1, keepdims=True)
                   + w_prior * jnp.sum(q.astype(F32) * n_old, axis=1, keepdims=True))
            hh = num / jnp.maximum(jnp.abs(den), jnp.exp(-m_t))
            m_new = m_t[CHUNK - 1:CHUNK, :]
            f_last = fc[CHUNK - 1:CHUNK, :]
            a = jnp.exp(f_last + m_prev - m_new)
            ws = jnp.exp(f_last - fc + igc - m_new) * K_SCALE
            kw = k.astype(F32) * ws
            c_s[b, h] = a * c_old + jnp.dot(kw.T.astype(BF16), v, preferred_element_type=F32)
            n_s[b, h:h + 1, :] = a * n_old + jnp.sum(kw, axis=0, keepdims=True)
            m_s[b, h:h + 1, :] = jnp.broadcast_to(m_new, (1, LANES))
            hn = _rms(hh) * mng_ref[:, h * ML_DK:(h + 1) * ML_DK]
            hn = hn * jax.nn.sigmoid(og.astype(F32))
            hm_s[out_rows, h * ML_DK:(h + 1) * ML_DK] = hn

    def unit(u, carry):
        if nch == 1:
            g, c = u, 0
        elif nb == SEQ_PAR:
            g, c = 0, u
        else:
            g, c = u // nch, u % nch
        for j in range(SEQ_PAR):
            chunk(g * SEQ_PAR + j, c)
        return carry

    lax.fori_loop(0, (nb // SEQ_PAR) * nch, unit, 0)

    mix = (jnp.dot(co, wout_ref[0:CONV_W, :], preferred_element_type=F32)
           + jnp.dot(hm_s[...].astype(BF16), wout_ref[CONV_W:CONV_W + ML_W, :], preferred_element_type=F32))
    g1 = mod_ref[:, :, 2 * d:3 * d]
    sh2 = mod_ref[:, :, 3 * d:4 * d]
    sc2 = mod_ref[:, :, 4 * d:5 * d]
    xn = x_ref[...] + g1 * mix.reshape(nb, ts, d)
    xo_ref[...] = xn
    h2 = _rms(xn) * n2g_ref[...] * (1.0 + sc2) + sh2
    h2_ref[...] = _pack_pairs(h2)
    logits_t = _dot3(rwt_ref[...], h2.reshape(m, d), NT_DIMS)
    for b in range(nb):
        lt_ref[b] = logits_t[:, b * ts:(b + 1) * ts]

    @pl.when(si == pl.num_programs(1) - 1)
    def _store_state():
        convo_ref[...] = last2
        co_ref[...] = c_s[...]
        no_ref[...] = n_s[...]
        mo_ref[...] = m_s[...]


def _mixer(x, mod, pc, pm, gcol, grow, conv0, c0, n0, m0, conv_w, cn_g, mn_g, w_out, n2_g, rw_t):
    batch, seq, d = x.shape
    nb, ts = _seq_tile(batch, seq, ROW_TILE // SEQ_PAR)
    nb = min(batch, nb * SEQ_PAR)
    assert nb % SEQ_PAR == 0 and batch % nb == 0
    nch = ts // CHUNK
    n_si = seq // ts
    grid = (batch // nb, n_si)
    tile = lambda w: pl.BlockSpec((nb, ts, w), lambda bi, si: (bi, si, 0))
    full = lambda a: pl.BlockSpec(a.shape, lambda bi, si: (0,) * a.ndim)
    per_b = lambda *tail: pl.BlockSpec((nb,) + tail, lambda bi, si: (bi,) + (0,) * len(tail))
    return pl.pallas_call(
        _mixer_kernel,
        grid=grid,
        in_specs=[tile(d), per_b(1, 6 * d), tile(3 * CONV_W), tile(4 * ML_W), tile(LANES),
                  pl.BlockSpec((nb, nch, N_GATE, CHUNK), lambda bi, si: (bi, si, 0, 0)),
                  per_b(CONV_K - 1, CONV_W), per_b(ML_H, ML_DK, ML_DK), per_b(8, LANES), per_b(8, LANES),
                  full(conv_w), full(cn_g), full(mn_g), full(w_out), full(n2_g), full(rw_t)],
        out_specs=[tile(d), tile(d // 2),
                   pl.BlockSpec((nb, N_EXPERTS, ts), lambda bi, si: (bi, 0, si)),
                   per_b(CONV_K - 1, CONV_W), per_b(ML_H, ML_DK, ML_DK), per_b(8, LANES), per_b(8, LANES)],
        out_shape=[jax.ShapeDtypeStruct((batch, seq, d), F32),
                   jax.ShapeDtypeStruct((batch, seq, d // 2), U32),
                   jax.ShapeDtypeStruct((batch, N_EXPERTS, seq), F32),
                   jax.ShapeDtypeStruct((batch, CONV_K - 1, CONV_W), F32),
                   jax.ShapeDtypeStruct((batch, ML_H, ML_DK, ML_DK), F32),
                   jax.ShapeDtypeStruct((batch, 8, LANES), F32),
                   jax.ShapeDtypeStruct((batch, 8, LANES), F32)],
        scratch_shapes=[pltpu.VMEM((nb, ts + 8, CONV_W), F32),
                        pltpu.VMEM((nb, ML_H, ML_DK, ML_DK), F32),
                        pltpu.VMEM((nb, 8, LANES), F32),
                        pltpu.VMEM((nb, 8, LANES), F32),
                        pltpu.VMEM((nb * ts, ML_W), F32)],
        compiler_params=_cparams("parallel", "arbitrary"),
        name="mixer",
    )(x, mod, pc, pm, gcol, grow, conv0, c0, n0, m0, conv_w, cn_g, mn_g, w_out, n2_g, rw_t)


def _route_kernel(lt_ref, rb_ref, tri_ref, wts_ref, lslot_ref, tcnt_ref, tbase_ref, cnt_ref, cnt_s):
    tt = lt_ref.shape[1]

    @pl.when(pl.program_id(0) == 0)
    def _zero():
        cnt_s[...] = jnp.zeros_like(cnt_s)

    neg = -jnp.inf
    scores = jax.nn.sigmoid(lt_ref[...])
    sel3 = (scores + rb_ref[...]).reshape(N_GROUPS, GROUP_SIZE, tt)
    sc3 = scores.reshape(N_GROUPS, GROUP_SIZE, tt)
    shape3 = (N_GROUPS, GROUP_SIZE, tt)
    j3 = lax.broadcasted_iota(I32, shape3, 1).astype(F32)
    g3 = lax.broadcasted_iota(I32, shape3, 0).astype(F32)
    e3 = g3 * GROUP_SIZE + j3
    m1 = jnp.max(sel3, axis=1, keepdims=True)
    first = jnp.min(jnp.where(sel3 == m1, j3, float(GROUP_SIZE)), axis=1, keepdims=True)
    m2 = jnp.max(jnp.where(j3 == first, neg, sel3), axis=1, keepdims=True)
    grp = m1 + m2
    gi = lax.broadcasted_iota(I32, grp.shape, 0).astype(F32)
    gmask = jnp.zeros(grp.shape, F32)
    for _ in range(TOPK_GROUPS):
        gmax = jnp.max(grp, axis=0, keepdims=True)
        gidx = jnp.min(jnp.where(grp == gmax, gi, float(N_GROUPS)), axis=0, keepdims=True)
        hit = gi == gidx
        gmask = jnp.where(hit, 1.0, gmask)
        grp = jnp.where(hit, neg, grp)
    cand = jnp.where(gmask > 0.0, sel3, neg)

    def red(fn, x):
        return fn(fn(x, axis=1, keepdims=True), axis=0, keepdims=True)

    selm = jnp.zeros(shape3, F32)
    idxs, ws = [], []
    for _ in range(TOP_K):
        cmax = red(jnp.max, cand)
        eidx = red(jnp.min, jnp.where(cand == cmax, e3, float(N_EXPERTS)))
        hit = e3 == eidx
        idxs.append(eidx)
        ws.append(red(jnp.sum, jnp.where(hit, sc3, 0.0)))
        selm = jnp.where(hit, 1.0, selm)
        cand = jnp.where(hit, neg, cand)
    wsum = ws[0]
    for k in range(1, TOP_K):
        wsum = wsum + ws[k]
    sel2 = selm.reshape(N_EXPERTS, tt)
    lcum = jnp.dot(sel2.astype(BF16), tri_ref[...], preferred_element_type=F32)
    base = cnt_s[...]
    tcnt = jnp.ceil(jnp.sum(sel2, axis=1, keepdims=True) * (1.0 / ROW_GROUP)) * ROW_GROUP
    tcnt = jnp.broadcast_to(tcnt, (N_EXPERTS, LANES))
    r_e = lax.broadcasted_iota(I32, (N_EXPERTS, N_EXPERTS), 0)
    c_e = lax.broadcasted_iota(I32, (N_EXPERTS, N_EXPERTS), 1)
    loff = jnp.dot((c_e < r_e).astype(BF16), tcnt.astype(BF16), preferred_element_type=F32)
    lsl3 = (lcum + loff[:, 0:1]).reshape(shape3)
    for k in range(TOP_K):
        hit = e3 == idxs[k]
        wts_ref[k:k + 1, :] = (ws[k] / wsum * ROUTED_SCALE).reshape(1, tt)
        lslot_ref[k:k + 1, :] = red(jnp.sum, jnp.where(hit, lsl3, 0.0)).reshape(1, tt).astype(I32)
    tcnt_ref[0] = tcnt
    tbase_ref[0] = base
    cnt_s[...] = base + tcnt
    cnt_ref[...] = cnt_s[...]


def _route(logits_t, router_b):
    tokens = logits_t.shape[1]
    tt = MOE_TILE
    n_tiles = tokens // tt
    r_i = lax.broadcasted_iota(I32, (tt, tt), 0)
    c_i = lax.broadcasted_iota(I32, (tt, tt), 1)
    tri = (r_i < c_i).astype(BF16)
    row = lambda: pl.BlockSpec((TOP_K, tt), lambda i: (0, i))
    stat = lambda: pl.BlockSpec((1, N_EXPERTS, LANES), lambda i: (i, 0, 0))
    return pl.pallas_call(
        _route_kernel,
        grid=(n_tiles,),
        in_specs=[pl.BlockSpec((N_EXPERTS, tt), lambda i: (0, i)),
                  pl.BlockSpec((N_EXPERTS, 1), lambda i: (0, 0)),
                  pl.BlockSpec((tt, tt), lambda i: (0, 0))],
        out_specs=[row(), row(), stat(), stat(),
                   pl.BlockSpec((N_EXPERTS, LANES), lambda i: (0, 0))],
        out_shape=[jax.ShapeDtypeStruct((TOP_K, tokens), F32),
                   jax.ShapeDtypeStruct((TOP_K, tokens), I32),
                   jax.ShapeDtypeStruct((n_tiles, N_EXPERTS, LANES), F32),
                   jax.ShapeDtypeStruct((n_tiles, N_EXPERTS, LANES), F32),
                   jax.ShapeDtypeStruct((N_EXPERTS, LANES), F32)],
        scratch_shapes=[pltpu.VMEM((N_EXPERTS, LANES), F32)],
        compiler_params=_cparams("arbitrary"),
        name="route",
    )(logits_t, router_b.reshape(N_EXPERTS, 1), tri)


MOE_TILE = 256
ROW_GROUP = 8
SLAB = 32
SORT_CHUNK = 512
FILL_ROWS = 64


def _dispatch_kernel(fill_ref, tab_ref, lslot_ref, h_ref, xs_hbm, xbuf, zbuf, pending, sem, fill_sem):
    tt, dw = h_ref.shape
    i = pl.program_id(0)
    slot = lax.rem(i, 2)
    n_chunks = xbuf.shape[1] // SORT_CHUNK

    @pl.when(i == 0)
    def _zero_fill():
        zbuf[...] = jnp.zeros_like(zbuf)

        def zero_piece(rows, dst):
            return pltpu.make_async_copy(zbuf.at[pl.ds(0, rows)], xs_hbm.at[pl.ds(dst, rows)], fill_sem)

        def run(n, rows, start, do_start):
            def body(s, carry):
                cp = zero_piece(rows, start + pl.multiple_of(s * rows, rows))
                if do_start:
                    cp.start()
                else:
                    cp.wait()
                return carry
            lax.fori_loop(0, n, body, 0)

        def per_expert(e, carry):
            start = pl.multiple_of(fill_ref[0, 0, e], ROW_GROUP)
            n_big = fill_ref[0, 0, N_EXPERTS + e]
            run(n_big, SLAB, start, True)
            run(fill_ref[0, 0, 2 * N_EXPERTS + e], ROW_GROUP,
                start + pl.multiple_of(n_big * SLAB, SLAB), True)
            return carry

        lax.fori_loop(0, N_EXPERTS, per_expert, 0)
        n_tail = fill_ref[0, 0, 3 * N_EXPERTS + 1]
        run(n_tail, FILL_ROWS, pl.multiple_of(fill_ref[0, 0, 3 * N_EXPERTS], FILL_ROWS), True)
        run(fill_ref[0, 0, 3 * N_EXPERTS + 2], SLAB, 0, False)
        run(fill_ref[0, 0, 3 * N_EXPERTS + 3], ROW_GROUP, 0, False)
        run(n_tail, FILL_ROWS, 0, False)

    lo, hi = _unpack_pairs(h_ref[...])
    lo = lo.astype(BF16)
    hi = hi.astype(BF16)
    lslot = lslot_ref[...].astype(jnp.int16)
    for c in range(n_chunks):
        r_i = lax.broadcasted_iota(jnp.int16, (SORT_CHUNK, tt), 0) + jnp.int16(c * SORT_CHUNK)
        perm = jnp.zeros((SORT_CHUNK, tt), BF16)
        for k in range(TOP_K):
            perm = jnp.where(r_i == lslot[k:k + 1, :], jnp.bfloat16(1.0), perm)
        x_lo = jnp.dot(perm, lo, preferred_element_type=F32)
        x_hi = jnp.dot(perm, hi, preferred_element_type=F32)
        xbuf[slot, c * SORT_CHUNK:(c + 1) * SORT_CHUNK, :] = _pack_pairs(
            jnp.concatenate([x_lo, x_hi], axis=-1))

    def piece(rows, sl, src, dst):
        return pltpu.make_async_copy(xbuf.at[sl, pl.ds(src, rows)], xs_hbm.at[pl.ds(dst, rows)], sem)

    def drain(n_big, n_small):
        def big(_, carry):
            piece(SLAB, 0, 0, 0).wait()
            return carry

        def small(_, carry):
            piece(ROW_GROUP, 0, 0, 0).wait()
            return carry

        lax.fori_loop(0, n_big, big, 0)
        lax.fori_loop(0, n_small, small, 0)

    @pl.when(i > 0)
    def _wait_previous():
        drain(pending[0], pending[1])

    def per_expert(e, carry):
        dst = pl.multiple_of(tab_ref[0, 0, e], ROW_GROUP)
        src = pl.multiple_of(tab_ref[0, 0, N_EXPERTS + e], ROW_GROUP)
        n_big = tab_ref[0, 0, 2 * N_EXPERTS + e]
        n_small = tab_ref[0, 0, 3 * N_EXPERTS + e]

        def big(s, c2):
            off = pl.multiple_of(s * SLAB, SLAB)
            piece(SLAB, slot, src + off, dst + off).start()
            return c2

        def small(s, c2):
            off = pl.multiple_of(n_big * SLAB + s * ROW_GROUP, ROW_GROUP)
            piece(ROW_GROUP, slot, src + off, dst + off).start()
            return c2

        lax.fori_loop(0, n_big, big, 0)
        lax.fori_loop(0, n_small, small, 0)
        return carry

    lax.fori_loop(0, N_EXPERTS, per_expert, 0)
    pending[0] = tab_ref[0, 0, 4 * N_EXPERTS]
    pending[1] = tab_ref[0, 0, 4 * N_EXPERTS + 1]

    @pl.when(i == pl.num_programs(0) - 1)
    def _wait_last():
        drain(pending[0], pending[1])


def _dispatch(fill, tab, lslot, h2p_flat, rows_total):
    tokens, dw = h2p_flat.shape
    tt = MOE_TILE
    local_rows = pl.cdiv(TOP_K * tt + N_EXPERTS * (ROW_GROUP - 1), SORT_CHUNK) * SORT_CHUNK
    return pl.pallas_call(
        _dispatch_kernel,
        grid=(tokens // tt,),
        in_specs=[pl.BlockSpec(fill.shape, lambda i: (0, 0, 0), memory_space=pltpu.SMEM),
                  pl.BlockSpec((1, 1, tab.shape[2]), lambda i: (i, 0, 0), memory_space=pltpu.SMEM),
                  pl.BlockSpec((TOP_K, tt), lambda i: (0, i)),
                  pl.BlockSpec((tt, dw), lambda i: (i, 0))],
        out_specs=pl.BlockSpec(memory_space=pl.ANY),
        out_shape=jax.ShapeDtypeStruct((rows_total, dw), h2p_flat.dtype),
        scratch_shapes=[pltpu.VMEM((2, local_rows, dw), U32), pltpu.VMEM((FILL_ROWS, dw), U32),
                        pltpu.SMEM((2,), I32), pltpu.SemaphoreType.DMA, pltpu.SemaphoreType.DMA],
        compiler_params=_cparams("arbitrary"),
        name="dispatch",
    )(fill, tab, lslot, h2p_flat)


def _swiglu_packed(xw, w1, w3, w2):
    lo, hi = _unpack_pairs(xw)
    x = jnp.concatenate([lo.astype(BF16), hi.astype(BF16)], axis=-1)
    a = jnp.dot(x, w1, preferred_element_type=F32)
    b = jnp.dot(x, w3, preferred_element_type=F32)
    hmid = (a * jax.nn.sigmoid(a) * b).astype(BF16)
    return jnp.dot(hmid, w2, preferred_element_type=F32)


def _expert_kernel(be_ref, nu_ref, x_ref, w1_ref, w3_ref, w2_ref, y_ref, w1_b, w3_b, w2_b):
    j = pl.program_id(0)

    @pl.when(j < nu_ref[0])
    def _compute():
        @pl.when(jnp.logical_or(j == 0, be_ref[j] != be_ref[jnp.maximum(j - 1, 0)]))
        def _cast_weights():
            w1_b[...] = w1_ref[0].astype(BF16)
            w3_b[...] = w3_ref[0].astype(BF16)
            w2_b[...] = w2_ref[0].astype(BF16)

        y_ref[...] = _pack_pairs(_swiglu_packed(x_ref[...], w1_b[...], w3_b[...], w2_b[...]))

    @pl.when(j >= nu_ref[0])
    def _unused():
        y_ref[...] = jnp.zeros_like(y_ref)


def _experts(block_e, n_used, xs, w1, w3, w2, rb):
    rows, dw = xs.shape
    nblk = rows // rb
    d, f = w1.shape[1:]
    return pl.pallas_call(
        _expert_kernel,
        grid_spec=pltpu.PrefetchScalarGridSpec(
            num_scalar_prefetch=2,
            grid=(nblk,),
            in_specs=[pl.BlockSpec((rb, dw), lambda j, be, nu: (jnp.minimum(j, nu[0] - 1), 0)),
                      pl.BlockSpec((1, d, f), lambda j, be, nu: (be[j], 0, 0)),
                      pl.BlockSpec((1, d, f), lambda j, be, nu: (be[j], 0, 0)),
                      pl.BlockSpec((1, f, d), lambda j, be, nu: (be[j], 0, 0))],
            out_specs=pl.BlockSpec((rb, dw), lambda j, be, nu: (j, 0)),
            scratch_shapes=[pltpu.VMEM((d, f), BF16), pltpu.VMEM((d, f), BF16), pltpu.VMEM((f, d), BF16)],
        ),
        out_shape=jax.ShapeDtypeStruct((rows, dw), U32),
        compiler_params=_cparams("arbitrary"),
        name="experts",
    )(block_e, n_used, xs, w1, w3, w2)


def _combine_kernel(tab0_ref, tabc_ref, tabn_ref, y_hbm, lslot_ref, wts_ref, h2_ref, x_ref, mod_ref,
                    sw1_ref, sw3_ref, sw2_ref, fg_ref, out_ref, ybuf, sem, *, final):
    nb, ts, d = x_ref.shape
    tc = nb * ts
    dw = d // 2
    i = pl.program_id(0)
    slot = lax.rem(i, 2)
    other = 1 - slot
    n_chunks = ybuf.shape[1] // SORT_CHUNK

    def piece(rows, sl, src, dst):
        return pltpu.make_async_copy(y_hbm.at[pl.ds(src, rows)], ybuf.at[sl, pl.ds(dst, rows)], sem.at[sl])

    def fetch(tab, sl):
        def per_expert(e, carry):
            src = pl.multiple_of(tab[0, 0, e], ROW_GROUP)
            dst = pl.multiple_of(tab[0, 0, N_EXPERTS + e], ROW_GROUP)
            n_big = tab[0, 0, 2 * N_EXPERTS + e]
            n_small = tab[0, 0, 3 * N_EXPERTS + e]

            def big(s, c2):
                off = pl.multiple_of(s * SLAB, SLAB)
                piece(SLAB, sl, src + off, dst + off).start()
                return c2

            def small(s, c2):
                off = pl.multiple_of(n_big * SLAB + s * ROW_GROUP, ROW_GROUP)
                piece(ROW_GROUP, sl, src + off, dst + off).start()
                return c2

            lax.fori_loop(0, n_big, big, 0)
            lax.fori_loop(0, n_small, small, 0)
            return carry

        lax.fori_loop(0, N_EXPERTS, per_expert, 0)

    def drain(tab, sl):
        def big(_, carry):
            piece(SLAB, sl, 0, 0).wait()
            return carry

        def small(_, carry):
            piece(ROW_GROUP, sl, 0, 0).wait()
            return carry

        lax.fori_loop(0, tab[0, 0, 4 * N_EXPERTS], big, 0)
        lax.fori_loop(0, tab[0, 0, 4 * N_EXPERTS + 1], small, 0)

    @pl.when(i == 0)
    def _first_tile():
        ybuf[...] = jnp.zeros_like(ybuf)
        fetch(tab0_ref, 0)

    fetch(tabn_ref, other)
    shared = _swiglu_packed(h2_ref[...].reshape(tc, dw), sw1_ref[...], sw3_ref[...], sw2_ref[...])
    drain(tabc_ref, slot)

    lslot = lslot_ref[...]
    wts = wts_ref[...]
    acc_lo = shared[:, :dw]
    acc_hi = shared[:, dw:]
    for c in range(n_chunks):
        r_i = lax.broadcasted_iota(I32, (SORT_CHUNK, tc), 0) + c * SORT_CHUNK
        wt = jnp.zeros((SORT_CHUNK, tc), F32)
        for k in range(TOP_K):
            wt = jnp.where(r_i == lslot[k:k + 1, :], wts[k:k + 1, :], wt)
        wm = wt.T.astype(BF16)
        lo, hi = _unpack_pairs(ybuf[slot, c * SORT_CHUNK:(c + 1) * SORT_CHUNK, :])
        acc_lo = acc_lo + jnp.dot(wm, lo.astype(BF16), preferred_element_type=F32)
        acc_hi = acc_hi + jnp.dot(wm, hi.astype(BF16), preferred_element_type=F32)
    acc = jnp.concatenate([acc_lo, acc_hi], axis=-1)
    xn = x_ref[...] + mod_ref[:, :, 5 * d:6 * d] * acc.reshape(nb, ts, d)
    if final:
        xn = _rms(xn) * fg_ref[...]
    out_ref[...] = xn

    @pl.when(i == pl.num_programs(0) - 1)
    def _last_tile():
        drain(tabn_ref, other)


def _combine(tab, y_sorted, lslot_t, wts_t, h2p, x, mod, sw1, sw3, sw2, final_g, final):
    batch, seq, d = x.shape
    dw = d // 2
    nb, ts = _seq_tile(batch, seq, MOE_TILE)
    tc = nb * ts
    n_si = seq // ts
    n_tiles = (batch // nb) * n_si
    local_rows = pl.cdiv(TOP_K * tc + N_EXPERTS * (ROW_GROUP - 1), SORT_CHUNK) * SORT_CHUNK
    tile = lambda w: pl.BlockSpec((nb, ts, w), lambda i: (i // n_si, i % n_si, 0))
    full = lambda a: pl.BlockSpec(a.shape, lambda i: (0,) * a.ndim)
    tab_spec = lambda f: pl.BlockSpec((1, 1, tab.shape[2]), lambda i: (f(i), 0, 0), memory_space=pltpu.SMEM)
    per_token = lambda: pl.BlockSpec((TOP_K, tc), lambda i: (0, i))
    return pl.pallas_call(
        functools.partial(_combine_kernel, final=final),
        grid=(n_tiles,),
        in_specs=[tab_spec(lambda i: 0), tab_spec(lambda i: i),
                  tab_spec(lambda i: jnp.minimum(i + 1, n_tiles - 1)),
                  pl.BlockSpec(memory_space=pl.ANY), per_token(), per_token(),
                  tile(dw), tile(d), pl.BlockSpec((nb, 1, 6 * d), lambda i: (i // n_si, 0, 0)),
                  full(sw1), full(sw3), full(sw2), full(final_g)],
        out_specs=tile(d),
        out_shape=jax.ShapeDtypeStruct((batch, seq, d), F32),
        scratch_shapes=[pltpu.VMEM((2, local_rows, dw), U32), pltpu.SemaphoreType.DMA((2,))],
        compiler_params=_cparams("arbitrary"),
        name="combine",
    )(tab, tab, tab, y_sorted, lslot_t, wts_t, h2p, x, mod, sw1, sw3, sw2, final_g)


def _expert_block_rows(tokens):
    if tokens >= 16384:
        return 1024
    if tokens >= 2048:
        return 256
    return 64


def _routing_tables(counts, tcnt, tbase, rb, nblk):
    counts = counts.astype(I32)
    padded = (counts + rb - 1) // rb * rb
    pends = jnp.cumsum(padded)
    pstarts = pends - padded
    block_start = jnp.arange(nblk, dtype=I32) * rb
    block_e = jnp.minimum(jnp.sum(pends[None, :] <= block_start[:, None], axis=1), N_EXPERTS - 1).astype(I32)
    n_used = (pends[-1:] // rb).astype(I32)
    tcnt = tcnt.astype(I32)
    first_row = pstarts[None, :] + tbase.astype(I32)
    local_off = jnp.cumsum(tcnt, axis=1) - tcnt
    n_big = tcnt // SLAB
    n_small = (tcnt - n_big * SLAB) // ROW_GROUP
    totals = jnp.stack([jnp.sum(n_big, axis=1), jnp.sum(n_small, axis=1)], axis=1)
    tab = jnp.concatenate([first_row, local_off, n_big, n_small, totals], axis=1)
    tab = jnp.pad(tab, ((0, 0), (0, 5 * N_EXPERTS - tab.shape[1])))[:, None, :]
    pad_rows = padded - counts
    pad_big = pad_rows // SLAB
    pad_small = (pad_rows - pad_big * SLAB) // ROW_GROUP
    tail = jnp.stack([pends[-1], (nblk * rb - pends[-1]) // FILL_ROWS, jnp.sum(pad_big), jnp.sum(pad_small)])
    fill = jnp.concatenate([pstarts + counts, pad_big, pad_small, tail])
    fill = jnp.pad(fill, (0, 4 * N_EXPERTS - fill.shape[0]))[None, None, :]
    return block_e, n_used, tab, fill


def _layer(x, mod, conv0, c0, n0, m0, p, final_g, final):
    batch, seq, d = x.shape
    tokens = batch * seq
    pc, pm, gcol, grow = _inproj(x, mod, p["norm1_g"], p["w_main"], p["wg_col"], p["b_col"])
    grow = grow.reshape(batch, N_GATE, seq // CHUNK, CHUNK).transpose(0, 2, 1, 3)
    m0b = jnp.broadcast_to(jnp.pad(m0, ((0, 0), (0, 8 - ML_H)))[:, :, None], (batch, 8, LANES))
    n0b = jnp.pad(n0, ((0, 0), (0, 8 - ML_H), (0, 0)))
    x1, h2, logits_t, conv_new, c_new, n_new, m_new = _mixer(
        x, mod, pc, pm, gcol, grow, conv0, c0, n0b, m0b, p["conv_w"], p["conv_norm_g"],
        p["mlstm_norm_g"], p["w_out"], p["norm2_g"], p["router_wt"])
    logits_t = logits_t.transpose(1, 0, 2).reshape(N_EXPERTS, tokens)
    wts, lslot, tcnt, tbase, cnt = _route(logits_t, p["router_b"])
    rb = _expert_block_rows(tokens)
    n_tiles = tokens // MOE_TILE
    nblk = pl.cdiv(tokens * TOP_K + n_tiles * N_EXPERTS * (ROW_GROUP - 1), rb) + N_EXPERTS
    block_e, n_used, tab, fill = _routing_tables(cnt[:, 0], tcnt[:, :, 0], tbase[:, :, 0], rb, nblk)
    xs = _dispatch(fill, tab, lslot, h2.reshape(tokens, d // 2), nblk * rb)
    ys = _experts(block_e, n_used, xs, p["exp_w1"], p["exp_w3"], p["exp_w2"], rb)
    out = _combine(tab, ys, lslot, wts, h2, x1, mod, p["shared_w1"], p["shared_w3"], p["shared_w2"],
                   final_g, final)
    return out, conv_new, c_new, n_new[:, :ML_H, :], m_new[:, :ML_H, 0]


def _trunk(x, mods, conv0, c0, n0, m0, layers, final_g):
    depth = len(layers)
    convs, cs, ns, ms = [], [], [], []
    for l in range(depth):
        x, cv, c, n, m = _layer(x, mods[l], conv0[l], c0[l], n0[l], m0[l], layers[l], final_g,
                                final=(l == depth - 1))
        convs.append(cv); cs.append(c); ns.append(n); ms.append(m)
    return x, jnp.stack(convs), jnp.stack(cs), jnp.stack(ns), jnp.stack(ms)


def kernel(x_prompt, x_sample, c_prompt, c_sample, cache_conv, state_mlstm_C, state_mlstm_n, state_mlstm_m,
           norm1_g, ada_w, ada_b, w_in, b_igate, b_fgate, conv_w, conv_norm_g, mlstm_norm_g, w_out, norm2_g,
           router_w, router_b, exp_w1, exp_w3, exp_w2, shared_w1, shared_w3, shared_w2, final_g):
    depth = w_in.shape[0]
    bp = x_prompt.shape[0]
    d = x_prompt.shape[-1]
    mods = _ada(jnp.concatenate([c_prompt, c_sample], axis=0), ada_w, ada_b)
    mods_p = mods[:, :bp, None, :]
    mods_s = mods[:, bp:, None, :]

    w_gate = w_in[:, :, N_MAIN:]
    bias = jnp.concatenate([b_igate, b_fgate], axis=-1)
    layers = []
    for l in range(depth):
        layers.append(dict(
            norm1_g=norm1_g[l][None, :],
            w_main=w_in[l, :, :N_MAIN].astype(BF16),
            wg_col=jnp.pad(w_gate[l], ((0, 0), (0, LANES - N_GATE))),
            b_col=jnp.pad(bias[l], (0, LANES - N_GATE))[None, :],
            conv_w=conv_w[l], conv_norm_g=conv_norm_g[l][None, :], mlstm_norm_g=mlstm_norm_g[l][None, :],
            w_out=w_out[l].astype(BF16), norm2_g=norm2_g[l][None, :],
            router_wt=router_w[l].T, router_b=router_b[l],
            exp_w1=exp_w1[l], exp_w3=exp_w3[l], exp_w2=exp_w2[l],
            shared_w1=shared_w1[l].astype(BF16), shared_w3=shared_w3[l].astype(BF16),
            shared_w2=shared_w2[l].astype(BF16)))
    fg = final_g[None, :]

    zeros = lambda *s: jnp.zeros((depth, bp) + s, x_prompt.dtype)
    y_p, conv_p, c_p, n_p, m_p = _trunk(
        x_prompt, mods_p, zeros(CONV_K - 1, CONV_W), zeros(ML_H, ML_DK, ML_DK), zeros(ML_H, ML_DK),
        zeros(ML_H), layers, fg)
    y_s, conv_s, c_s, n_s, m_s = _trunk(
        x_sample, mods_s, cache_conv, state_mlstm_C, state_mlstm_n, state_mlstm_m, layers, fg)
    return (y_p, y_s, conv_p, c_p, n_p, m_p, conv_s, c_s, n_s, m_s)
```

```python
import functools

import jax
import jax.numpy as jnp
from jax import lax
from jax.experimental import pallas as pl
from jax.experimental.pallas import tpu as pltpu

F32 = jnp.float32
BF16 = jnp.bfloat16
I32 = jnp.int32
U32 = jnp.uint32
HIGHEST = lax.Precision.HIGHEST

D_MODEL = 1024
CONV_W = 512
CONV_K = 3
ML_W = 512
ML_H = 4
ML_DK = 128
CHUNK = 64
N_EXPERTS = 64
TOP_K = 8
N_GROUPS = 8
GROUP_SIZE = N_EXPERTS // N_GROUPS
TOPK_GROUPS = 4
D_EXPERT = 256
ROUTED_SCALE = 2.5
EPS = 1e-6
K_SCALE = ML_DK ** -0.5
N_MAIN = 3 * CONV_W + 4 * ML_W
N_GATE = 2 * ML_H
LANES = 128
ROW_TILE = 512
VMEM_LIMIT = 56 * 1024 * 1024

NT_DIMS = (((1,), (1,)), ((), ()))


def _cparams(*sem):
    return pltpu.CompilerParams(dimension_semantics=sem, vmem_limit_bytes=VMEM_LIMIT)


def _rms(x):
    return x * lax.rsqrt(jnp.mean(x * x, axis=-1, keepdims=True) + EPS)


def _dot3(a, b, dims=None):
    def split(x):
        hi = x.astype(BF16)
        return hi, (x - hi.astype(F32)).astype(BF16)

    def dot(x, y):
        if dims is None:
            return jnp.dot(x, y, preferred_element_type=F32)
        return lax.dot_general(x, y, dims, preferred_element_type=F32)

    a_hi, a_lo = split(a)
    b_hi, b_lo = split(b)
    return dot(a_hi, b_hi) + dot(a_hi, b_lo) + dot(a_lo, b_hi)


def _log_sigmoid(x):
    return jnp.minimum(x, 0.0) - jnp.log1p(jnp.exp(-jnp.abs(x)))


def _pack_pairs(x):
    n = x.shape[-1] // 2
    packed = pltpu.pack_elementwise([x[..., :n], x[..., n:]], packed_dtype=BF16)
    return lax.bitcast_convert_type(packed, U32)


def _unpack_pairs(w):
    w = lax.bitcast_convert_type(w, I32)
    lo = pltpu.unpack_elementwise(w, index=0, packed_dtype=BF16, unpacked_dtype=F32)
    hi = pltpu.unpack_elementwise(w, index=1, packed_dtype=BF16, unpacked_dtype=F32)
    return lo, hi


def _seq_tile(batch, seq, rows):
    if seq >= rows:
        return 1, rows
    return min(batch, rows // seq), seq


def _ada_kernel(c_ref, w_ref, b_ref, o_ref):
    c = c_ref[...]
    s = (c * jax.nn.sigmoid(c)).astype(BF16)
    o_ref[0] = jnp.dot(s, w_ref[0].astype(BF16), preferred_element_type=F32) + b_ref[0]


def _ada(c_all, ada_w, ada_b):
    depth, d, n = ada_w.shape
    nb = c_all.shape[0]
    tn = 1536
    return pl.pallas_call(
        _ada_kernel,
        grid=(depth, n // tn),
        in_specs=[
            pl.BlockSpec((nb, d), lambda l, j: (0, 0)),
            pl.BlockSpec((1, d, tn), lambda l, j: (l, 0, j)),
            pl.BlockSpec((1, 1, tn), lambda l, j: (l, 0, j)),
        ],
        out_specs=pl.BlockSpec((1, nb, tn), lambda l, j: (l, 0, j)),
        out_shape=jax.ShapeDtypeStruct((depth, nb, n), F32),
        compiler_params=_cparams("parallel", "parallel"),
        name="ada",
    )(c_all, ada_w, ada_b.reshape(depth, 1, n))


def _inproj_kernel(x_ref, mod_ref, g_ref, wm_ref, wgc_ref, bc_ref, pc_ref, pm_ref, gc_ref, gr_ref):
    nb, ts, d = x_ref.shape
    m = nb * ts
    sh = mod_ref[:, :, 0:d]
    sc = mod_ref[:, :, d:2 * d]
    h = (_rms(x_ref[...]) * g_ref[...] * (1.0 + sc) + sh).reshape(m, d)
    hb = h.astype(BF16)
    cn = 512
    for j in range(N_MAIN // cn):
        pj = jnp.dot(hb, wm_ref[:, j * cn:(j + 1) * cn], preferred_element_type=F32).astype(BF16)
        if j < 3:
            pc_ref[:, :, j * cn:(j + 1) * cn] = pj.reshape(nb, ts, cn)
        else:
            pm_ref[:, :, (j - 3) * cn:(j - 2) * cn] = pj.reshape(nb, ts, cn)
    gcol = _dot3(h, wgc_ref[...]) + bc_ref[...]
    lane = lax.broadcasted_iota(I32, gcol.shape, 1)
    gcol = jnp.where(lane >= ML_H, _log_sigmoid(gcol), gcol)
    gc_ref[...] = gcol.reshape(nb, ts, LANES)
    grow = gcol.T[:N_GATE]
    for b in range(nb):
        gr_ref[b] = grow[:, b * ts:(b + 1) * ts]


def _inproj(x, mod, norm_g, w_main, wg_col, b_col):
    batch, seq, d = x.shape
    nb, ts = _seq_tile(batch, seq, ROW_TILE)
    grid = (batch // nb, seq // ts)
    tile = lambda w: pl.BlockSpec((nb, ts, w), lambda bi, si: (bi, si, 0))
    full = lambda a: pl.BlockSpec(a.shape, lambda bi, si: (0,) * a.ndim)
    return pl.pallas_call(
        _inproj_kernel,
        grid=grid,
        in_specs=[tile(d), pl.BlockSpec((nb, 1, 6 * d), lambda bi, si: (bi, 0, 0)),
                  full(norm_g), full(w_main), full(wg_col), full(b_col)],
        out_specs=[tile(3 * CONV_W), tile(4 * ML_W), tile(LANES),
                   pl.BlockSpec((nb, N_GATE, ts), lambda bi, si: (bi, 0, si))],
        out_shape=[jax.ShapeDtypeStruct((batch, seq, 3 * CONV_W), BF16),
                   jax.ShapeDtypeStruct((batch, seq, 4 * ML_W), BF16),
                   jax.ShapeDtypeStruct((batch, seq, LANES), F32),
                   jax.ShapeDtypeStruct((batch, N_GATE, seq), F32)],
        compiler_params=_cparams("parallel", "parallel"),
        name="inproj",
    )(x, mod, norm_g, w_main, wg_col, b_col)


SEQ_PAR = 4


def _mixer_kernel(x_ref, mod_ref, pc_ref, pm_ref, gc_ref, gr_ref, conv0_ref, c0_ref, n0_ref, m0_ref,
                  convw_ref, cng_ref, mng_ref, wout_ref, n2g_ref, rwt_ref,
                  xo_ref, h2_ref, lt_ref, convo_ref, co_ref, no_ref, mo_ref,
                  ubuf, c_s, n_s, m_s, hm_s):
    nb, ts, d = x_ref.shape
    nch = ts // CHUNK
    m = nb * ts
    si = pl.program_id(1)
    hdr = 8

    @pl.when(si == 0)
    def _load_state():
        ubuf[:, hdr - 2:hdr, :] = conv0_ref[...]
        c_s[...] = c0_ref[...]
        n_s[...] = n0_ref[...]
        m_s[...] = m0_ref[...]

    cb = pc_ref[:, :, 0:CONV_W].astype(F32)
    cc = pc_ref[:, :, CONV_W:2 * CONV_W].astype(F32)
    ch = pc_ref[:, :, 2 * CONV_W:3 * CONV_W].astype(F32)
    ubuf[:, hdr:hdr + ts, :] = cc * ch
    w = convw_ref[...]
    y = (w[0:1] * ubuf[:, hdr - 2:hdr - 2 + ts, :] + w[1:2] * ubuf[:, hdr - 1:hdr - 1 + ts, :]
         + w[2:3] * ubuf[:, hdr:hdr + ts, :])
    co = (_rms(cb * y) * cng_ref[...]).reshape(m, CONV_W).astype(BF16)
    last2 = ubuf[:, hdr + ts - 2:hdr + ts, :]
    ubuf[:, hdr - 2:hdr, :] = last2

    r_i = lax.broadcasted_iota(I32, (CHUNK, CHUNK), 0)
    c_i = lax.broadcasted_iota(I32, (CHUNK, CHUNK), 1)
    causal = r_i >= c_i
    ltri = causal.astype(F32)
    utri = (r_i <= c_i).astype(F32)

    def chunk(b, c):
        t0 = pl.multiple_of(c * CHUNK, CHUNK)
        rows = pl.ds(t0, CHUNK)
        out_rows = pl.ds(pl.multiple_of(b * ts + t0, CHUNK), CHUNK)
        gcol = gc_ref[b, rows, :]
        grow = gr_ref[b, c]
        f_col = jnp.dot(ltri, gcol, precision=HIGHEST, preferred_element_type=F32)
        f_row = jnp.dot(grow, utri, precision=HIGHEST, preferred_element_type=F32)
        for h in range(ML_H):
            q = pm_ref[b, rows, h * ML_DK:(h + 1) * ML_DK]
            k = pm_ref[b, rows, ML_W + h * ML_DK:ML_W + (h + 1) * ML_DK]
            v = pm_ref[b, rows, 2 * ML_W + h * ML_DK:2 * ML_W + (h + 1) * ML_DK]
            og = pm_ref[b, rows, 3 * ML_W + h * ML_DK:3 * ML_W + (h + 1) * ML_DK]
            fc = f_col[:, ML_H + h:ML_H + h + 1]
            fr = f_row[ML_H + h:ML_H + h + 1, :]
            igc = gcol[:, h:h + 1]
            igr = grow[h:h + 1, :]
            dmat = jnp.where(causal, fc - fr + igr, -jnp.inf)
            m_prev = m_s[b, h:h + 1, 0:1]
            prior = fc + m_prev
            m_t = jnp.maximum(prior, jnp.max(dmat, axis=1, keepdims=True))
            w_prior = jnp.exp(prior - m_t)
            p = jnp.exp(dmat - m_t)
            qk = lax.dot_general(q, k, NT_DIMS, preferred_element_type=F32)
            s = qk * (p * K_SCALE)
            c_old = c_s[b, h]
            n_old = n_s[b, h:h + 1, :]
            num = (jnp.dot(s.astype(BF16), v, preferred_element_type=F32)
                   + w_prior * jnp.dot(q, c_old.astype(BF16), preferred_element_type=F32))
            den = (jnp.sum(s, axis=1, keepdims=True)
                   + w_prior * jnp.sum(q.astype(F32) * n_old, axis=1, keepdims=True))
            hh = num / jnp.maximum(jnp.abs(den), jnp.exp(-m_t))
            m_new = m_t[CHUNK - 1:CHUNK, :]
            f_last = fc[CHUNK - 1:CHUNK, :]
            a = jnp.exp(f_last + m_prev - m_new)
            ws = jnp.exp(f_last - fc + igc - m_new) * K_SCALE
            kw = k.astype(F32) * ws
            c_s[b, h] = a * c_old + jnp.dot(kw.T.astype(BF16), v, preferred_element_type=F32)
            n_s[b, h:h + 1, :] = a * n_old + jnp.sum(kw, axis=0, keepdims=True)
            m_s[b, h:h + 1, :] = jnp.broadcast_to(m_new, (1, LANES))
            hn = _rms(hh) * mng_ref[:, h * ML_DK:(h + 1) * ML_DK]
            hn = hn * jax.nn.sigmoid(og.astype(F32))
            hm_s[out_rows, h * ML_DK:(h + 1) * ML_DK] = hn

    def unit(u, carry):
        if nch == 1:
            g, c = u, 0
        elif nb == SEQ_PAR:
            g, c = 0, u
        else:
            g, c = u // nch, u % nch
        for j in range(SEQ_PAR):
            chunk(g * SEQ_PAR + j, c)
        return carry

    lax.fori_loop(0, (nb // SEQ_PAR) * nch, unit, 0)

    mix = (jnp.dot(co, wout_ref[0:CONV_W, :], preferred_element_type=F32)
           + jnp.dot(hm_s[...].astype(BF16), wout_ref[CONV_W:CONV_W + ML_W, :], preferred_element_type=F32))
    g1 = mod_ref[:, :, 2 * d:3 * d]
    sh2 = mod_ref[:, :, 3 * d:4 * d]
    sc2 = mod_ref[:, :, 4 * d:5 * d]
    xn = x_ref[...] + g1 * mix.reshape(nb, ts, d)
    xo_ref[...] = xn
    h2 = _rms(xn) * n2g_ref[...] * (1.0 + sc2) + sh2
    h2_ref[...] = _pack_pairs(h2)
    logits_t = _dot3(rwt_ref[...], h2.reshape(m, d), NT_DIMS)
    for b in range(nb):
        lt_ref[b] = logits_t[:, b * ts:(b + 1) * ts]

    @pl.when(si == pl.num_programs(1) - 1)
    def _store_state():
        convo_ref[...] = last2
        co_ref[...] = c_s[...]
        no_ref[...] = n_s[...]
        mo_ref[...] = m_s[...]


def _mixer(x, mod, pc, pm, gcol, grow, conv0, c0, n0, m0, conv_w, cn_g, mn_g, w_out, n2_g, rw_t):
    batch, seq, d = x.shape
    nb, ts = _seq_tile(batch, seq, ROW_TILE // SEQ_PAR)
    nb = min(batch, nb * SEQ_PAR)
    assert nb % SEQ_PAR == 0 and batch % nb == 0
    nch = ts // CHUNK
    n_si = seq // ts
    grid = (batch // nb, n_si)
    tile = lambda w: pl.BlockSpec((nb, ts, w), lambda bi, si: (bi, si, 0))
    full = lambda a: pl.BlockSpec(a.shape, lambda bi, si: (0,) * a.ndim)
    per_b = lambda *tail: pl.BlockSpec((nb,) + tail, lambda bi, si: (bi,) + (0,) * len(tail))
    return pl.pallas_call(
        _mixer_kernel,
        grid=grid,
        in_specs=[tile(d), per_b(1, 6 * d), tile(3 * CONV_W), tile(4 * ML_W), tile(LANES),
                  pl.BlockSpec((nb, nch, N_GATE, CHUNK), lambda bi, si: (bi, si, 0, 0)),
                  per_b(CONV_K - 1, CONV_W), per_b(ML_H, ML_DK, ML_DK), per_b(8, LANES), per_b(8, LANES),
                  full(conv_w), full(cn_g), full(mn_g), full(w_out), full(n2_g), full(rw_t)],
        out_specs=[tile(d), tile(d // 2),
                   pl.BlockSpec((nb, N_EXPERTS, ts), lambda bi, si: (bi, 0, si)),
                   per_b(CONV_K - 1, CONV_W), per_b(ML_H, ML_DK, ML_DK), per_b(8, LANES), per_b(8, LANES)],
        out_shape=[jax.ShapeDtypeStruct((batch, seq, d), F32),
                   jax.ShapeDtypeStruct((batch, seq, d // 2), U32),
                   jax.ShapeDtypeStruct((batch, N_EXPERTS, seq), F32),
                   jax.ShapeDtypeStruct((batch, CONV_K - 1, CONV_W), F32),
                   jax.ShapeDtypeStruct((batch, ML_H, ML_DK, ML_DK), F32),
                   jax.ShapeDtypeStruct((batch, 8, LANES), F32),
                   jax.ShapeDtypeStruct((batch, 8, LANES), F32)],
        scratch_shapes=[pltpu.VMEM((nb, ts + 8, CONV_W), F32),
                        pltpu.VMEM((nb, ML_H, ML_DK, ML_DK), F32),
                        pltpu.VMEM((nb, 8, LANES), F32),
                        pltpu.VMEM((nb, 8, LANES), F32),
                        pltpu.VMEM((nb * ts, ML_W), F32)],
        compiler_params=_cparams("parallel", "arbitrary"),
        name="mixer",
    )(x, mod, pc, pm, gcol, grow, conv0, c0, n0, m0, conv_w, cn_g, mn_g, w_out, n2_g, rw_t)


def _route_kernel(lt_ref, rb_ref, tri_ref, wts_ref, lslot_ref, tcnt_ref, tbase_ref, cnt_ref, cnt_s):
    tt = lt_ref.shape[1]

    @pl.when(pl.program_id(0) == 0)
    def _zero():
        cnt_s[...] = jnp.zeros_like(cnt_s)

    neg = -jnp.inf
    scores = jax.nn.sigmoid(lt_ref[...])
    sel3 = (scores + rb_ref[...]).reshape(N_GROUPS, GROUP_SIZE, tt)
    sc3 = scores.reshape(N_GROUPS, GROUP_SIZE, tt)
    shape3 = (N_GROUPS, GROUP_SIZE, tt)
    j3 = lax.broadcasted_iota(I32, shape3, 1).astype(F32)
    g3 = lax.broadcasted_iota(I32, shape3, 0).astype(F32)
    e3 = g3 * GROUP_SIZE + j3
    m1 = jnp.max(sel3, axis=1, keepdims=True)
    first = jnp.min(jnp.where(sel3 == m1, j3, float(GROUP_SIZE)), axis=1, keepdims=True)
    m2 = jnp.max(jnp.where(j3 == first, neg, sel3), axis=1, keepdims=True)
    grp = m1 + m2
    gi = lax.broadcasted_iota(I32, grp.shape, 0).astype(F32)
    gmask = jnp.zeros(grp.shape, F32)
    for _ in range(TOPK_GROUPS):
        gmax = jnp.max(grp, axis=0, keepdims=True)
        gidx = jnp.min(jnp.where(grp == gmax, gi, float(N_GROUPS)), axis=0, keepdims=True)
        hit = gi == gidx
        gmask = jnp.where(hit, 1.0, gmask)
        grp = jnp.where(hit, neg, grp)
    cand = jnp.where(gmask > 0.0, sel3, neg)

    def red(fn, x):
        return fn(fn(x, axis=1, keepdims=True), axis=0, keepdims=True)

    selm = jnp.zeros(shape3, F32)
    idxs, ws = [], []
    for _ in range(TOP_K):
        cmax = red(jnp.max, cand)
        eidx = red(jnp.min, jnp.where(cand == cmax, e3, float(N_EXPERTS)))
        hit = e3 == eidx
        idxs.append(eidx)
        ws.append(red(jnp.sum, jnp.where(hit, sc3, 0.0)))
        selm = jnp.where(hit, 1.0, selm)
        cand = jnp.where(hit, neg, cand)
    wsum = ws[0]
    for k in range(1, TOP_K):
        wsum = wsum + ws[k]
    sel2 = selm.reshape(N_EXPERTS, tt)
    lcum = jnp.dot(sel2.astype(BF16), tri_ref[...], preferred_element_type=F32)
    base = cnt_s[...]
    tcnt = jnp.ceil(jnp.sum(sel2, axis=1, keepdims=True) * (1.0 / ROW_GROUP)) * ROW_GROUP
    tcnt = jnp.broadcast_to(tcnt, (N_EXPERTS, LANES))
    r_e = lax.broadcasted_iota(I32, (N_EXPERTS, N_EXPERTS), 0)
    c_e = lax.broadcasted_iota(I32, (N_EXPERTS, N_EXPERTS), 1)
    loff = jnp.dot((c_e < r_e).astype(BF16), tcnt.astype(BF16), preferred_element_type=F32)
    lsl3 = (lcum + loff[:, 0:1]).reshape(shape3)
    for k in range(TOP_K):
        hit = e3 == idxs[k]
        wts_ref[k:k + 1, :] = (ws[k] / wsum * ROUTED_SCALE).reshape(1, tt)
        lslot_ref[k:k + 1, :] = red(jnp.sum, jnp.where(hit, lsl3, 0.0)).reshape(1, tt).astype(I32)
    tcnt_ref[0] = tcnt
    tbase_ref[0] = base
    cnt_s[...] = base + tcnt
    cnt_ref[...] = cnt_s[...]


def _route(logits_t, router_b):
    tokens = logits_t.shape[1]
    tt = MOE_TILE
    n_tiles = tokens // tt
    r_i = lax.broadcasted_iota(I32, (tt, tt), 0)
    c_i = lax.broadcasted_iota(I32, (tt, tt), 1)
    tri = (r_i < c_i).astype(BF16)
    row = lambda: pl.BlockSpec((TOP_K, tt), lambda i: (0, i))
    stat = lambda: pl.BlockSpec((1, N_EXPERTS, LANES), lambda i: (i, 0, 0))
    return pl.pallas_call(
        _route_kernel,
        grid=(n_tiles,),
        in_specs=[pl.BlockSpec((N_EXPERTS, tt), lambda i: (0, i)),
                  pl.BlockSpec((N_EXPERTS, 1), lambda i: (0, 0)),
                  pl.BlockSpec((tt, tt), lambda i: (0, 0))],
        out_specs=[row(), row(), stat(), stat(),
                   pl.BlockSpec((N_EXPERTS, LANES), lambda i: (0, 0))],
        out_shape=[jax.ShapeDtypeStruct((TOP_K, tokens), F32),
                   jax.ShapeDtypeStruct((TOP_K, tokens), I32),
                   jax.ShapeDtypeStruct((n_tiles, N_EXPERTS, LANES), F32),
                   jax.ShapeDtypeStruct((n_tiles, N_EXPERTS, LANES), F32),
                   jax.ShapeDtypeStruct((N_EXPERTS, LANES), F32)],
        scratch_shapes=[pltpu.VMEM((N_EXPERTS, LANES), F32)],
        compiler_params=_cparams("arbitrary"),
        name="route",
    )(logits_t, router_b.reshape(N_EXPERTS, 1), tri)


MOE_TILE = 256
ROW_GROUP = 8
SLAB = 32
SORT_CHUNK = 512
FILL_ROWS = 64


def _dispatch_kernel(fill_ref, tab_ref, lslot_ref, h_ref, xs_hbm, xbuf, zbuf, pending, sem, fill_sem):
    tt, dw = h_ref.shape
    i = pl.program_id(0)
    slot = lax.rem(i, 2)
    n_chunks = xbuf.shape[1] // SORT_CHUNK

    @pl.when(i == 0)
    def _zero_fill():
        zbuf[...] = jnp.zeros_like(zbuf)

        def zero_piece(rows, dst):
            return pltpu.make_async_copy(zbuf.at[pl.ds(0, rows)], xs_hbm.at[pl.ds(dst, rows)], fill_sem)

        def run(n, rows, start, do_start):
            def body(s, carry):
                cp = zero_piece(rows, start + pl.multiple_of(s * rows, rows))
                if do_start:
                    cp.start()
                else:
                    cp.wait()
                return carry
            lax.fori_loop(0, n, body, 0)

        def per_expert(e, carry):
            start = pl.multiple_of(fill_ref[0, 0, e], ROW_GROUP)
            n_big = fill_ref[0, 0, N_EXPERTS + e]
            run(n_big, SLAB, start, True)
            run(fill_ref[0, 0, 2 * N_EXPERTS + e], ROW_GROUP,
                start + pl.multiple_of(n_big * SLAB, SLAB), True)
            return carry

        lax.fori_loop(0, N_EXPERTS, per_expert, 0)
        n_tail = fill_ref[0, 0, 3 * N_EXPERTS + 1]
        run(n_tail, FILL_ROWS, pl.multiple_of(fill_ref[0, 0, 3 * N_EXPERTS], FILL_ROWS), True)
        run(fill_ref[0, 0, 3 * N_EXPERTS + 2], SLAB, 0, False)
        run(fill_ref[0, 0, 3 * N_EXPERTS + 3], ROW_GROUP, 0, False)
        run(n_tail, FILL_ROWS, 0, False)

    lo, hi = _unpack_pairs(h_ref[...])
    lo = lo.astype(BF16)
    hi = hi.astype(BF16)
    lslot = lslot_ref[...].astype(jnp.int16)
    for c in range(n_chunks):
        r_i = lax.broadcasted_iota(jnp.int16, (SORT_CHUNK, tt), 0) + jnp.int16(c * SORT_CHUNK)
        perm = jnp.zeros((SORT_CHUNK, tt), BF16)
        for k in range(TOP_K):
            perm = jnp.where(r_i == lslot[k:k + 1, :], jnp.bfloat16(1.0), perm)
        x_lo = jnp.dot(perm, lo, preferred_element_type=F32)
        x_hi = jnp.dot(perm, hi, preferred_element_type=F32)
        xbuf[slot, c * SORT_CHUNK:(c + 1) * SORT_CHUNK, :] = _pack_pairs(
            jnp.concatenate([x_lo, x_hi], axis=-1))

    def piece(rows, sl, src, dst):
        return pltpu.make_async_copy(xbuf.at[sl, pl.ds(src, rows)], xs_hbm.at[pl.ds(dst, rows)], sem)

    def drain(n_big, n_small):
        def big(_, carry):
            piece(SLAB, 0, 0, 0).wait()
            return carry

        def small(_, carry):
            piece(ROW_GROUP, 0, 0, 0).wait()
            return carry

        lax.fori_loop(0, n_big, big, 0)
        lax.fori_loop(0, n_small, small, 0)

    @pl.when(i > 0)
    def _wait_previous():
        drain(pending[0], pending[1])

    def per_expert(e, carry):
        dst = pl.multiple_of(tab_ref[0, 0, e], ROW_GROUP)
        src = pl.multiple_of(tab_ref[0, 0, N_EXPERTS + e], ROW_GROUP)
        n_big = tab_ref[0, 0, 2 * N_EXPERTS + e]
        n_small = tab_ref[0, 0, 3 * N_EXPERTS + e]

        def big(s, c2):
            off = pl.multiple_of(s * SLAB, SLAB)
            piece(SLAB, slot, src + off, dst + off).start()
            return c2

        def small(s, c2):
            off = pl.multiple_of(n_big * SLAB + s * ROW_GROUP, ROW_GROUP)
            piece(ROW_GROUP, slot, src + off, dst + off).start()
            return c2

        lax.fori_loop(0, n_big, big, 0)
        lax.fori_loop(0, n_small, small, 0)
        return carry

    lax.fori_loop(0, N_EXPERTS, per_expert, 0)
    pending[0] = tab_ref[0, 0, 4 * N_EXPERTS]
    pending[1] = tab_ref[0, 0, 4 * N_EXPERTS + 1]

    @pl.when(i == pl.num_programs(0) - 1)
    def _wait_last():
        drain(pending[0], pending[1])


def _dispatch(fill, tab, lslot, h2p_flat, rows_total):
    tokens, dw = h2p_flat.shape
    tt = MOE_TILE
    local_rows = pl.cdiv(TOP_K * tt + N_EXPERTS * (ROW_GROUP - 1), SORT_CHUNK) * SORT_CHUNK
    return pl.pallas_call(
        _dispatch_kernel,
        grid=(tokens // tt,),
        in_specs=[pl.BlockSpec(fill.shape, lambda i: (0, 0, 0), memory_space=pltpu.SMEM),
                  pl.BlockSpec((1, 1, tab.shape[2]), lambda i: (i, 0, 0), memory_space=pltpu.SMEM),
                  pl.BlockSpec((TOP_K, tt), lambda i: (0, i)),
                  pl.BlockSpec((tt, dw), lambda i: (i, 0))],
        out_specs=pl.BlockSpec(memory_space=pl.ANY),
        out_shape=jax.ShapeDtypeStruct((rows_total, dw), h2p_flat.dtype),
        scratch_shapes=[pltpu.VMEM((2, local_rows, dw), U32), pltpu.VMEM((FILL_ROWS, dw), U32),
                        pltpu.SMEM((2,), I32), pltpu.SemaphoreType.DMA, pltpu.SemaphoreType.DMA],
        compiler_params=_cparams("arbitrary"),
        name="dispatch",
    )(fill, tab, lslot, h2p_flat)


def _swiglu_packed(xw, w1, w3, w2):
    lo, hi = _unpack_pairs(xw)
    x = jnp.concatenate([lo.astype(BF16), hi.astype(BF16)], axis=-1)
    a = jnp.dot(x, w1, preferred_element_type=F32)
    b = jnp.dot(x, w3, preferred_element_type=F32)
    hmid = (a * jax.nn.sigmoid(a) * b).astype(BF16)
    return jnp.dot(hmid, w2, preferred_element_type=F32)


def _expert_kernel(be_ref, nu_ref, x_ref, w1_ref, w3_ref, w2_ref, y_ref, w1_b, w3_b, w2_b):
    j = pl.program_id(0)

    @pl.when(j < nu_ref[0])
    def _compute():
        @pl.when(jnp.logical_or(j == 0, be_ref[j] != be_ref[jnp.maximum(j - 1, 0)]))
        def _cast_weights():
            w1_b[...] = w1_ref[0].astype(BF16)
            w3_b[...] = w3_ref[0].astype(BF16)
            w2_b[...] = w2_ref[0].astype(BF16)

        y_ref[...] = _pack_pairs(_swiglu_packed(x_ref[...], w1_b[...], w3_b[...], w2_b[...]))

    @pl.when(j >= nu_ref[0])
    def _unused():
        y_ref[...] = jnp.zeros_like(y_ref)


def _experts(block_e, n_used, xs, w1, w3, w2, rb):
    rows, dw = xs.shape
    nblk = rows // rb
    d, f = w1.shape[1:]
    return pl.pallas_call(
        _expert_kernel,
        grid_spec=pltpu.PrefetchScalarGridSpec(
            num_scalar_prefetch=2,
            grid=(nblk,),
            in_specs=[pl.BlockSpec((rb, dw), lambda j, be, nu: (jnp.minimum(j, nu[0] - 1), 0)),
                      pl.BlockSpec((1, d, f), lambda j, be, nu: (be[j], 0, 0)),
                      pl.BlockSpec((1, d, f), lambda j, be, nu: (be[j], 0, 0)),
                      pl.BlockSpec((1, f, d), lambda j, be, nu: (be[j], 0, 0))],
            out_specs=pl.BlockSpec((rb, dw), lambda j, be, nu: (j, 0)),
            scratch_shapes=[pltpu.VMEM((d, f), BF16), pltpu.VMEM((d, f), BF16), pltpu.VMEM((f, d), BF16)],
        ),
        out_shape=jax.ShapeDtypeStruct((rows, dw), U32),
        compiler_params=_cparams("arbitrary"),
        name="experts",
    )(block_e, n_used, xs, w1, w3, w2)


def _combine_kernel(tab0_ref, tabc_ref, tabn_ref, y_hbm, lslot_ref, wts_ref, h2_ref, x_ref, mod_ref,
                    sw1_ref, sw3_ref, sw2_ref, fg_ref, out_ref, ybuf, sem, *, final):
    nb, ts, d = x_ref.shape
    tc = nb * ts
    dw = d // 2
    i = pl.program_id(0)
    slot = lax.rem(i, 2)
    other = 1 - slot
    n_chunks = ybuf.shape[1] // SORT_CHUNK

    def piece(rows, sl, src, dst):
        return pltpu.make_async_copy(y_hbm.at[pl.ds(src, rows)], ybuf.at[sl, pl.ds(dst, rows)], sem.at[sl])

    def fetch(tab, sl):
        def per_expert(e, carry):
            src = pl.multiple_of(tab[0, 0, e], ROW_GROUP)
            dst = pl.multiple_of(tab[0, 0, N_EXPERTS + e], ROW_GROUP)
            n_big = tab[0, 0, 2 * N_EXPERTS + e]
            n_small = tab[0, 0, 3 * N_EXPERTS + e]

            def big(s, c2):
                off = pl.multiple_of(s * SLAB, SLAB)
                piece(SLAB, sl, src + off, dst + off).start()
                return c2

            def small(s, c2):
                off = pl.multiple_of(n_big * SLAB + s * ROW_GROUP, ROW_GROUP)
                piece(ROW_GROUP, sl, src + off, dst + off).start()
                return c2

            lax.fori_loop(0, n_big, big, 0)
            lax.fori_loop(0, n_small, small, 0)
            return carry

        lax.fori_loop(0, N_EXPERTS, per_expert, 0)

    def drain(tab, sl):
        def big(_, carry):
            piece(SLAB, sl, 0, 0).wait()
            return carry

        def small(_, carry):
            piece(ROW_GROUP, sl, 0, 0).wait()
            return carry

        lax.fori_loop(0, tab[0, 0, 4 * N_EXPERTS], big, 0)
        lax.fori_loop(0, tab[0, 0, 4 * N_EXPERTS + 1], small, 0)

    @pl.when(i == 0)
    def _first_tile():
        ybuf[...] = jnp.zeros_like(ybuf)
        fetch(tab0_ref, 0)

    fetch(tabn_ref, other)
    shared = _swiglu_packed(h2_ref[...].reshape(tc, dw), sw1_ref[...], sw3_ref[...], sw2_ref[...])
    drain(tabc_ref, slot)

    lslot = lslot_ref[...].astype(jnp.int16)
    wts = wts_ref[...].astype(BF16)
    acc_lo = shared[:, :dw]
    acc_hi = shared[:, dw:]
    for c in range(n_chunks):
        r_i = lax.broadcasted_iota(jnp.int16, (SORT_CHUNK, tc), 0) + jnp.int16(c * SORT_CHUNK)
        wt = jnp.zeros((SORT_CHUNK, tc), BF16)
        for k in range(TOP_K):
            wt = jnp.where(r_i == lslot[k:k + 1, :], wts[k:k + 1, :], wt)
        wm = wt.astype(F32).T.astype(BF16)
        lo, hi = _unpack_pairs(ybuf[slot, c * SORT_CHUNK:(c + 1) * SORT_CHUNK, :])
        acc_lo = acc_lo + jnp.dot(wm, lo.astype(BF16), preferred_element_type=F32)
        acc_hi = acc_hi + jnp.dot(wm, hi.astype(BF16), preferred_element_type=F32)
    acc = jnp.concatenate([acc_lo, acc_hi], axis=-1)
    xn = x_ref[...] + mod_ref[:, :, 5 * d:6 * d] * acc.reshape(nb, ts, d)
    if final:
        xn = _rms(xn) * fg_ref[...]
    out_ref[...] = xn

    @pl.when(i == pl.num_programs(0) - 1)
    def _last_tile():
        drain(tabn_ref, other)


def _combine(tab, y_sorted, lslot_t, wts_t, h2p, x, mod, sw1, sw3, sw2, final_g, final):
    batch, seq, d = x.shape
    dw = d // 2
    nb, ts = _seq_tile(batch, seq, MOE_TILE)
    tc = nb * ts
    n_si = seq // ts
    n_tiles = (batch // nb) * n_si
    local_rows = pl.cdiv(TOP_K * tc + N_EXPERTS * (ROW_GROUP - 1), SORT_CHUNK) * SORT_CHUNK
    tile = lambda w: pl.BlockSpec((nb, ts, w), lambda i: (i // n_si, i % n_si, 0))
    full = lambda a: pl.BlockSpec(a.shape, lambda i: (0,) * a.ndim)
    tab_spec = lambda f: pl.BlockSpec((1, 1, tab.shape[2]), lambda i: (f(i), 0, 0), memory_space=pltpu.SMEM)
    per_token = lambda: pl.BlockSpec((TOP_K, tc), lambda i: (0, i))
    return pl.pallas_call(
        functools.partial(_combine_kernel, final=final),
        grid=(n_tiles,),
        in_specs=[tab_spec(lambda i: 0), tab_spec(lambda i: i),
                  tab_spec(lambda i: jnp.minimum(i + 1, n_tiles - 1)),
                  pl.BlockSpec(memory_space=pl.ANY), per_token(), per_token(),
                  tile(dw), tile(d), pl.BlockSpec((nb, 1, 6 * d), lambda i: (i // n_si, 0, 0)),
                  full(sw1), full(sw3), full(sw2), full(final_g)],
        out_specs=tile(d),
        out_shape=jax.ShapeDtypeStruct((batch, seq, d), F32),
        scratch_shapes=[pltpu.VMEM((2, local_rows, dw), U32), pltpu.SemaphoreType.DMA((2,))],
        compiler_params=_cparams("arbitrary"),
        name="combine",
    )(tab, tab, tab, y_sorted, lslot_t, wts_t, h2p, x, mod, sw1, sw3, sw2, final_g)


def _expert_block_rows(tokens):
    if tokens >= 16384:
        return 1024
    if tokens >= 2048:
        return 256
    return 64


def _routing_tables(counts, tcnt, tbase, rb, nblk):
    counts = counts.astype(I32)
    padded = (counts + rb - 1) // rb * rb
    pends = jnp.cumsum(padded)
    pstarts = pends - padded
    block_start = jnp.arange(nblk, dtype=I32) * rb
    block_e = jnp.minimum(jnp.sum(pends[None, :] <= block_start[:, None], axis=1), N_EXPERTS - 1).astype(I32)
    n_used = (pends[-1:] // rb).astype(I32)
    tcnt = tcnt.astype(I32)
    first_row = pstarts[None, :] + tbase.astype(I32)
    local_off = jnp.cumsum(tcnt, axis=1) - tcnt
    n_big = tcnt // SLAB
    n_small = (tcnt - n_big * SLAB) // ROW_GROUP
    totals = jnp.stack([jnp.sum(n_big, axis=1), jnp.sum(n_small, axis=1)], axis=1)
    tab = jnp.concatenate([first_row, local_off, n_big, n_small, totals], axis=1)
    tab = jnp.pad(tab, ((0, 0), (0, 5 * N_EXPERTS - tab.shape[1])))[:, None, :]
    pad_rows = padded - counts
    pad_big = pad_rows // SLAB
    pad_small = (pad_rows - pad_big * SLAB) // ROW_GROUP
    tail = jnp.stack([pends[-1], (nblk * rb - pends[-1]) // FILL_ROWS, jnp.sum(pad_big), jnp.sum(pad_small)])
    fill = jnp.concatenate([pstarts + counts, pad_big, pad_small, tail])
    fill = jnp.pad(fill, (0, 4 * N_EXPERTS - fill.shape[0]))[None, None, :]
    return block_e, n_used, tab, fill


def _layer(x, mod, conv0, c0, n0, m0, p, final_g, final):
    batch, seq, d = x.shape
    tokens = batch * seq
    pc, pm, gcol, grow = _inproj(x, mod, p["norm1_g"], p["w_main"], p["wg_col"], p["b_col"])
    grow = grow.reshape(batch, N_GATE, seq // CHUNK, CHUNK).transpose(0, 2, 1, 3)
    m0b = jnp.broadcast_to(jnp.pad(m0, ((0, 0), (0, 8 - ML_H)))[:, :, None], (batch, 8, LANES))
    n0b = jnp.pad(n0, ((0, 0), (0, 8 - ML_H), (0, 0)))
    x1, h2, logits_t, conv_new, c_new, n_new, m_new = _mixer(
        x, mod, pc, pm, gcol, grow, conv0, c0, n0b, m0b, p["conv_w"], p["conv_norm_g"],
        p["mlstm_norm_g"], p["w_out"], p["norm2_g"], p["router_wt"])
    logits_t = logits_t.transpose(1, 0, 2).reshape(N_EXPERTS, tokens)
    wts, lslot, tcnt, tbase, cnt = _route(logits_t, p["router_b"])
    rb = _expert_block_rows(tokens)
    n_tiles = tokens // MOE_TILE
    nblk = pl.cdiv(tokens * TOP_K + n_tiles * N_EXPERTS * (ROW_GROUP - 1), rb) + N_EXPERTS
    block_e, n_used, tab, fill = _routing_tables(cnt[:, 0], tcnt[:, :, 0], tbase[:, :, 0], rb, nblk)
    xs = _dispatch(fill, tab, lslot, h2.reshape(tokens, d // 2), nblk * rb)
    ys = _experts(block_e, n_used, xs, p["exp_w1"], p["exp_w3"], p["exp_w2"], rb)
    out = _combine(tab, ys, lslot, wts, h2, x1, mod, p["shared_w1"], p["shared_w3"], p["shared_w2"],
                   final_g, final)
    return out, conv_new, c_new, n_new[:, :ML_H, :], m_new[:, :ML_H, 0]


def _trunk(x, mods, conv0, c0, n0, m0, layers, final_g):
    depth = len(layers)
    convs, cs, ns, ms = [], [], [], []
    for l in range(depth):
        x, cv, c, n, m = _layer(x, mods[l], conv0[l], c0[l], n0[l], m0[l], layers[l], final_g,
                                final=(l == depth - 1))
        convs.append(cv); cs.append(c); ns.append(n); ms.append(m)
    return x, jnp.stack(convs), jnp.stack(cs), jnp.stack(ns), jnp.stack(ms)


def kernel(x_prompt, x_sample, c_prompt, c_sample, cache_conv, state_mlstm_C, state_mlstm_n, state_mlstm_m,
           norm1_g, ada_w, ada_b, w_in, b_igate, b_fgate, conv_w, conv_norm_g, mlstm_norm_g, w_out, norm2_g,
           router_w, router_b, exp_w1, exp_w3, exp_w2, shared_w1, shared_w3, shared_w2, final_g):
    depth = w_in.shape[0]
    bp = x_prompt.shape[0]
    d = x_prompt.shape[-1]
    mods = _ada(jnp.concatenate([c_prompt, c_sample], axis=0), ada_w, ada_b)
    mods_p = mods[:, :bp, None, :]
    mods_s = mods[:, bp:, None, :]

    w_gate = w_in[:, :, N_MAIN:]
    bias = jnp.concatenate([b_igate, b_fgate], axis=-1)
    layers = []
    for l in range(depth):
        layers.append(dict(
            norm1_g=norm1_g[l][None, :],
            w_main=w_in[l, :, :N_MAIN].astype(BF16),
            wg_col=jnp.pad(w_gate[l], ((0, 0), (0, LANES - N_GATE))),
            b_col=jnp.pad(bias[l], (0, LANES - N_GATE))[None, :],
            conv_w=conv_w[l], conv_norm_g=conv_norm_g[l][None, :], mlstm_norm_g=mlstm_norm_g[l][None, :],
            w_out=w_out[l].astype(BF16), norm2_g=norm2_g[l][None, :],
            router_wt=router_w[l].T, router_b=router_b[l],
            exp_w1=exp_w1[l], exp_w3=exp_w3[l], exp_w2=exp_w2[l],
            shared_w1=shared_w1[l].astype(BF16), shared_w3=shared_w3[l].astype(BF16),
            shared_w2=shared_w2[l].astype(BF16)))
    fg = final_g[None, :]

    zeros = lambda *s: jnp.zeros((depth, bp) + s, x_prompt.dtype)
    y_p, conv_p, c_p, n_p, m_p = _trunk(
        x_prompt, mods_p, zeros(CONV_K - 1, CONV_W), zeros(ML_H, ML_DK, ML_DK), zeros(ML_H, ML_DK),
        zeros(ML_H), layers, fg)
    y_s, conv_s, c_s, n_s, m_s = _trunk(
        x_sample, mods_s, cache_conv, state_mlstm_C, state_mlstm_n, state_mlstm_m, layers, fg)
    return (y_p, y_s, conv_p, c_p, n_p, m_p, conv_s, c_s, n_s, m_s)
```

```python
import functools

import jax
import jax.numpy as jnp
from jax import lax
from jax.experimental import pallas as pl
from jax.experimental.pallas import tpu as pltpu

F32 = jnp.float32
BF16 = jnp.bfloat16
I32 = jnp.int32
U32 = jnp.uint32
HIGHEST = lax.Precision.HIGHEST

D_MODEL = 1024
CONV_W = 512
CONV_K = 3
ML_W = 512
ML_H = 4
ML_DK = 128
CHUNK = 64
N_EXPERTS = 64
TOP_K = 8
N_GROUPS = 8
GROUP_SIZE = N_EXPERTS // N_GROUPS
TOPK_GROUPS = 4
D_EXPERT = 256
ROUTED_SCALE = 2.5
EPS = 1e-6
K_SCALE = ML_DK ** -0.5
N_MAIN = 3 * CONV_W + 4 * ML_W
N_GATE = 2 * ML_H
LANES = 128
ROW_TILE = 512
VMEM_LIMIT = 56 * 1024 * 1024

NT_DIMS = (((1,), (1,)), ((), ()))


def _cparams(*sem):
    return pltpu.CompilerParams(dimension_semantics=sem, vmem_limit_bytes=VMEM_LIMIT)


def _rms(x):
    return x * lax.rsqrt(jnp.mean(x * x, axis=-1, keepdims=True) + EPS)


def _dot3(a, b, dims=None):
    def split(x):
        hi = x.astype(BF16)
        return hi, (x - hi.astype(F32)).astype(BF16)

    def dot(x, y):
        if dims is None:
            return jnp.dot(x, y, preferred_element_type=F32)
        return lax.dot_general(x, y, dims, preferred_element_type=F32)

    a_hi, a_lo = split(a)
    b_hi, b_lo = split(b)
    return dot(a_hi, b_hi) + dot(a_hi, b_lo) + dot(a_lo, b_hi)


def _log_sigmoid(x):
    return jnp.minimum(x, 0.0) - jnp.log1p(jnp.exp(-jnp.abs(x)))


def _pack_pairs(x):
    n = x.shape[-1] // 2
    packed = pltpu.pack_elementwise([x[..., :n], x[..., n:]], packed_dtype=BF16)
    return lax.bitcast_convert_type(packed, U32)


def _unpack_pairs(w):
    w = lax.bitcast_convert_type(w, I32)
    lo = pltpu.unpack_elementwise(w, index=0, packed_dtype=BF16, unpacked_dtype=F32)
    hi = pltpu.unpack_elementwise(w, index=1, packed_dtype=BF16, unpacked_dtype=F32)
    return lo, hi


def _seq_tile(batch, seq, rows):
    if seq >= rows:
        return 1, rows
    return min(batch, rows // seq), seq


def _ada_kernel(c_ref, w_ref, b_ref, o_ref):
    c = c_ref[...]
    s = (c * jax.nn.sigmoid(c)).astype(BF16)
    o_ref[0] = jnp.dot(s, w_ref[0].astype(BF16), preferred_element_type=F32) + b_ref[0]


def _ada(c_all, ada_w, ada_b):
    depth, d, n = ada_w.shape
    nb = c_all.shape[0]
    tn = 1536
    return pl.pallas_call(
        _ada_kernel,
        grid=(depth, n // tn),
        in_specs=[
            pl.BlockSpec((nb, d), lambda l, j: (0, 0)),
            pl.BlockSpec((1, d, tn), lambda l, j: (l, 0, j)),
            pl.BlockSpec((1, 1, tn), lambda l, j: (l, 0, j)),
        ],
        out_specs=pl.BlockSpec((1, nb, tn), lambda l, j: (l, 0, j)),
        out_shape=jax.ShapeDtypeStruct((depth, nb, n), F32),
        compiler_params=_cparams("parallel", "parallel"),
        name="ada",
    )(c_all, ada_w, ada_b.reshape(depth, 1, n))


def _inproj_kernel(x_ref, mod_ref, g_ref, wm_ref, wgc_ref, bc_ref, pc_ref, pm_ref, gc_ref, gr_ref):
    nb, ts, d = x_ref.shape
    m = nb * ts
    sh = mod_ref[:, :, 0:d]
    sc = mod_ref[:, :, d:2 * d]
    h = (_rms(x_ref[...]) * g_ref[...] * (1.0 + sc) + sh).reshape(m, d)
    hb = h.astype(BF16)
    cn = 512
    for j in range(N_MAIN // cn):
        pj = jnp.dot(hb, wm_ref[:, j * cn:(j + 1) * cn], preferred_element_type=F32).astype(BF16)
        if j < 3:
            pc_ref[:, :, j * cn:(j + 1) * cn] = pj.reshape(nb, ts, cn)
        else:
            pm_ref[:, :, (j - 3) * cn:(j - 2) * cn] = pj.reshape(nb, ts, cn)
    gcol = _dot3(h, wgc_ref[...]) + bc_ref[...]
    lane = lax.broadcasted_iota(I32, gcol.shape, 1)
    gcol = jnp.where(lane >= ML_H, _log_sigmoid(gcol), gcol)
    gc_ref[...] = gcol.reshape(nb, ts, LANES)
    grow = gcol.T[:N_GATE]
    for b in range(nb):
        gr_ref[b] = grow[:, b * ts:(b + 1) * ts]


def _inproj(x, mod, norm_g, w_main, wg_col, b_col):
    batch, seq, d = x.shape
    nb, ts = _seq_tile(batch, seq, ROW_TILE)
    grid = (batch // nb, seq // ts)
    tile = lambda w: pl.BlockSpec((nb, ts, w), lambda bi, si: (bi, si, 0))
    full = lambda a: pl.BlockSpec(a.shape, lambda bi, si: (0,) * a.ndim)
    return pl.pallas_call(
        _inproj_kernel,
        grid=grid,
        in_specs=[tile(d), pl.BlockSpec((nb, 1, 6 * d), lambda bi, si: (bi, 0, 0)),
                  full(norm_g), full(w_main), full(wg_col), full(b_col)],
        out_specs=[tile(3 * CONV_W), tile(4 * ML_W), tile(LANES),
                   pl.BlockSpec((nb, N_GATE, ts), lambda bi, si: (bi, 0, si))],
        out_shape=[jax.ShapeDtypeStruct((batch, seq, 3 * CONV_W), BF16),
                   jax.ShapeDtypeStruct((batch, seq, 4 * ML_W), BF16),
                   jax.ShapeDtypeStruct((batch, seq, LANES), F32),
                   jax.ShapeDtypeStruct((batch, N_GATE, seq), F32)],
        compiler_params=_cparams("parallel", "parallel"),
        name="inproj",
    )(x, mod, norm_g, w_main, wg_col, b_col)


SEQ_PAR = 4


def _mixer_kernel(x_ref, mod_ref, pc_ref, pm_ref, gc_ref, gr_ref, conv0_ref, c0_ref, n0_ref, m0_ref,
                  convw_ref, cng_ref, mng_ref, wout_ref, n2g_ref, rwt_ref,
                  xo_ref, h2_ref, lt_ref, convo_ref, co_ref, no_ref, mo_ref,
                  ubuf, c_s, n_s, m_s, hm_s):
    nb, ts, d = x_ref.shape
    nch = ts // CHUNK
    m = nb * ts
    si = pl.program_id(1)
    hdr = 8

    @pl.when(si == 0)
    def _load_state():
        ubuf[:, hdr - 2:hdr, :] = conv0_ref[...]
        c_s[...] = c0_ref[...]
        n_s[...] = n0_ref[...]
        m_s[...] = m0_ref[...]

    cb = pc_ref[:, :, 0:CONV_W].astype(F32)
    cc = pc_ref[:, :, CONV_W:2 * CONV_W].astype(F32)
    ch = pc_ref[:, :, 2 * CONV_W:3 * CONV_W].astype(F32)
    ubuf[:, hdr:hdr + ts, :] = cc * ch
    w = convw_ref[...]
    y = (w[0:1] * ubuf[:, hdr - 2:hdr - 2 + ts, :] + w[1:2] * ubuf[:, hdr - 1:hdr - 1 + ts, :]
         + w[2:3] * ubuf[:, hdr:hdr + ts, :])
    co = (_rms(cb * y) * cng_ref[...]).reshape(m, CONV_W).astype(BF16)
    last2 = ubuf[:, hdr + ts - 2:hdr + ts, :]
    ubuf[:, hdr - 2:hdr, :] = last2

    r_i = lax.broadcasted_iota(I32, (CHUNK, CHUNK), 0)
    c_i = lax.broadcasted_iota(I32, (CHUNK, CHUNK), 1)
    causal = r_i >= c_i
    ltri = causal.astype(F32)
    utri = (r_i <= c_i).astype(F32)

    def chunk(b, c):
        t0 = pl.multiple_of(c * CHUNK, CHUNK)
        rows = pl.ds(t0, CHUNK)
        out_rows = pl.ds(pl.multiple_of(b * ts + t0, CHUNK), CHUNK)
        gcol = gc_ref[b, rows, :]
        grow = gr_ref[b, c]
        f_col = jnp.dot(ltri, gcol, precision=HIGHEST, preferred_element_type=F32)
        f_row = jnp.dot(grow, utri, precision=HIGHEST, preferred_element_type=F32)
        for h in range(ML_H):
            q = pm_ref[b, rows, h * ML_DK:(h + 1) * ML_DK]
            k = pm_ref[b, rows, ML_W + h * ML_DK:ML_W + (h + 1) * ML_DK]
            v = pm_ref[b, rows, 2 * ML_W + h * ML_DK:2 * ML_W + (h + 1) * ML_DK]
            og = pm_ref[b, rows, 3 * ML_W + h * ML_DK:3 * ML_W + (h + 1) * ML_DK]
            fc = f_col[:, ML_H + h:ML_H + h + 1]
            fr = f_row[ML_H + h:ML_H + h + 1, :]
            igc = gcol[:, h:h + 1]
            igr = grow[h:h + 1, :]
            dmat = jnp.where(causal, fc - fr + igr, -jnp.inf)
            m_prev = m_s[b, h:h + 1, 0:1]
            prior = fc + m_prev
            m_t = jnp.maximum(prior, jnp.max(dmat, axis=1, keepdims=True))
            w_prior = jnp.exp(prior - m_t)
            p = jnp.exp(dmat - m_t)
            qk = lax.dot_general(q, k, NT_DIMS, preferred_element_type=F32)
            s = qk * (p * K_SCALE)
            c_old = c_s[b, h]
            n_old = n_s[b, h:h + 1, :]
            num = (jnp.dot(s.astype(BF16), v, preferred_element_type=F32)
                   + w_prior * jnp.dot(q, c_old.astype(BF16), preferred_element_type=F32))
            den = (jnp.sum(s, axis=1, keepdims=True)
                   + w_prior * jnp.sum(q.astype(F32) * n_old, axis=1, keepdims=True))
            hh = num / jnp.maximum(jnp.abs(den), jnp.exp(-m_t))
            m_new = m_t[CHUNK - 1:CHUNK, :]
            f_last = fc[CHUNK - 1:CHUNK, :]
            a = jnp.exp(f_last + m_prev - m_new)
            ws = jnp.exp(f_last - fc + igc - m_new) * K_SCALE
            kw = k.astype(F32) * ws
            c_s[b, h] = a * c_old + jnp.dot(kw.T.astype(BF16), v, preferred_element_type=F32)
            n_s[b, h:h + 1, :] = a * n_old + jnp.sum(kw, axis=0, keepdims=True)
            m_s[b, h:h + 1, :] = jnp.broadcast_to(m_new, (1, LANES))
            hn = _rms(hh) * mng_ref[:, h * ML_DK:(h + 1) * ML_DK]
            hn = hn * jax.nn.sigmoid(og.astype(F32))
            hm_s[out_rows, h * ML_DK:(h + 1) * ML_DK] = hn

    def unit(u, carry):
        if nch == 1:
            g, c = u, 0
        elif nb == SEQ_PAR:
            g, c = 0, u
        else:
            g, c = u // nch, u % nch
        for j in range(SEQ_PAR):
            chunk(g * SEQ_PAR + j, c)
        return carry

    lax.fori_loop(0, (nb // SEQ_PAR) * nch, unit, 0)

    mix = (jnp.dot(co, wout_ref[0:CONV_W, :], preferred_element_type=F32)
           + jnp.dot(hm_s[...].astype(BF16), wout_ref[CONV_W:CONV_W + ML_W, :], preferred_element_type=F32))
    g1 = mod_ref[:, :, 2 * d:3 * d]
    sh2 = mod_ref[:, :, 3 * d:4 * d]
    sc2 = mod_ref[:, :, 4 * d:5 * d]
    xn = x_ref[...] + g1 * mix.reshape(nb, ts, d)
    xo_ref[...] = xn
    h2 = _rms(xn) * n2g_ref[...] * (1.0 + sc2) + sh2
    h2_ref[...] = _pack_pairs(h2)
    logits_t = _dot3(rwt_ref[...], h2.reshape(m, d), NT_DIMS)
    for b in range(nb):
        lt_ref[b] = logits_t[:, b * ts:(b + 1) * ts]

    @pl.when(si == pl.num_programs(1) - 1)
    def _store_state():
        convo_ref[...] = last2
        co_ref[...] = c_s[...]
        no_ref[...] = n_s[...]
        mo_ref[...] = m_s[...]


def _mixer(x, mod, pc, pm, gcol, grow, conv0, c0, n0, m0, conv_w, cn_g, mn_g, w_out, n2_g, rw_t):
    batch, seq, d = x.shape
    nb, ts = _seq_tile(batch, seq, ROW_TILE // SEQ_PAR)
    nb = min(batch, nb * SEQ_PAR)
    assert nb % SEQ_PAR == 0 and batch % nb == 0
    nch = ts // CHUNK
    n_si = seq // ts
    grid = (batch // nb, n_si)
    tile = lambda w: pl.BlockSpec((nb, ts, w), lambda bi, si: (bi, si, 0))
    full = lambda a: pl.BlockSpec(a.shape, lambda bi, si: (0,) * a.ndim)
    per_b = lambda *tail: pl.BlockSpec((nb,) + tail, lambda bi, si: (bi,) + (0,) * len(tail))
    return pl.pallas_call(
        _mixer_kernel,
        grid=grid,
        in_specs=[tile(d), per_b(1, 6 * d), tile(3 * CONV_W), tile(4 * ML_W), tile(LANES),
                  pl.BlockSpec((nb, nch, N_GATE, CHUNK), lambda bi, si: (bi, si, 0, 0)),
                  per_b(CONV_K - 1, CONV_W), per_b(ML_H, ML_DK, ML_DK), per_b(8, LANES), per_b(8, LANES),
                  full(conv_w), full(cn_g), full(mn_g), full(w_out), full(n2_g), full(rw_t)],
        out_specs=[tile(d), tile(d // 2),
                   pl.BlockSpec((nb, N_EXPERTS, ts), lambda bi, si: (bi, 0, si)),
                   per_b(CONV_K - 1, CONV_W), per_b(ML_H, ML_DK, ML_DK), per_b(8, LANES), per_b(8, LANES)],
        out_shape=[jax.ShapeDtypeStruct((batch, seq, d), F32),
                   jax.ShapeDtypeStruct((batch, seq, d // 2), U32),
                   jax.ShapeDtypeStruct((batch, N_EXPERTS, seq), F32),
                   jax.ShapeDtypeStruct((batch, CONV_K - 1, CONV_W), F32),
                   jax.ShapeDtypeStruct((batch, ML_H, ML_DK, ML_DK), F32),
                   jax.ShapeDtypeStruct((batch, 8, LANES), F32),
                   jax.ShapeDtypeStruct((batch, 8, LANES), F32)],
        scratch_shapes=[pltpu.VMEM((nb, ts + 8, CONV_W), F32),
                        pltpu.VMEM((nb, ML_H, ML_DK, ML_DK), F32),
                        pltpu.VMEM((nb, 8, LANES), F32),
                        pltpu.VMEM((nb, 8, LANES), F32),
                        pltpu.VMEM((nb * ts, ML_W), F32)],
        compiler_params=_cparams("parallel", "arbitrary"),
        name="mixer",
    )(x, mod, pc, pm, gcol, grow, conv0, c0, n0, m0, conv_w, cn_g, mn_g, w_out, n2_g, rw_t)


def _route_kernel(lt_ref, rb_ref, tri_ref, wts_ref, lslot_ref, tcnt_ref, tbase_ref, cnt_ref, cnt_s):
    tt = lt_ref.shape[1]

    @pl.when(pl.program_id(0) == 0)
    def _zero():
        cnt_s[...] = jnp.zeros_like(cnt_s)

    neg = -jnp.inf
    scores = jax.nn.sigmoid(lt_ref[...])
    sel3 = (scores + rb_ref[...]).reshape(N_GROUPS, GROUP_SIZE, tt)
    sc3 = scores.reshape(N_GROUPS, GROUP_SIZE, tt)
    shape3 = (N_GROUPS, GROUP_SIZE, tt)
    j3 = lax.broadcasted_iota(I32, shape3, 1).astype(F32)
    g3 = lax.broadcasted_iota(I32, shape3, 0).astype(F32)
    e3 = g3 * GROUP_SIZE + j3
    m1 = jnp.max(sel3, axis=1, keepdims=True)
    first = jnp.min(jnp.where(sel3 == m1, j3, float(GROUP_SIZE)), axis=1, keepdims=True)
    m2 = jnp.max(jnp.where(j3 == first, neg, sel3), axis=1, keepdims=True)
    grp = m1 + m2
    gi = lax.broadcasted_iota(I32, grp.shape, 0).astype(F32)
    gmask = jnp.zeros(grp.shape, F32)
    for _ in range(TOPK_GROUPS):
        gmax = jnp.max(grp, axis=0, keepdims=True)
        gidx = jnp.min(jnp.where(grp == gmax, gi, float(N_GROUPS)), axis=0, keepdims=True)
        hit = gi == gidx
        gmask = jnp.where(hit, 1.0, gmask)
        grp = jnp.where(hit, neg, grp)
    cand = jnp.where(gmask > 0.0, sel3, neg)

    def red(fn, x):
        return fn(fn(x, axis=1, keepdims=True), axis=0, keepdims=True)

    selm = jnp.zeros(shape3, F32)
    idxs, ws = [], []
    for _ in range(TOP_K):
        cmax = red(jnp.max, cand)
        eidx = red(jnp.min, jnp.where(cand == cmax, e3, float(N_EXPERTS)))
        hit = e3 == eidx
        idxs.append(eidx)
        ws.append(red(jnp.sum, jnp.where(hit, sc3, 0.0)))
        selm = jnp.where(hit, 1.0, selm)
        cand = jnp.where(hit, neg, cand)
    wsum = ws[0]
    for k in range(1, TOP_K):
        wsum = wsum + ws[k]
    sel2 = selm.reshape(N_EXPERTS, tt)
    lcum = jnp.dot(sel2.astype(BF16), tri_ref[...], preferred_element_type=F32)
    base = cnt_s[...]
    tcnt = jnp.ceil(jnp.sum(sel2, axis=1, keepdims=True) * (1.0 / ROW_GROUP)) * ROW_GROUP
    tcnt = jnp.broadcast_to(tcnt, (N_EXPERTS, LANES))
    r_e = lax.broadcasted_iota(I32, (N_EXPERTS, N_EXPERTS), 0)
    c_e = lax.broadcasted_iota(I32, (N_EXPERTS, N_EXPERTS), 1)
    loff = jnp.dot((c_e < r_e).astype(BF16), tcnt.astype(BF16), preferred_element_type=F32)
    lsl3 = (lcum + loff[:, 0:1]).reshape(shape3)
    for k in range(TOP_K):
        hit = e3 == idxs[k]
        wts_ref[k:k + 1, :] = (ws[k] / wsum * ROUTED_SCALE).reshape(1, tt)
        lslot_ref[k:k + 1, :] = red(jnp.sum, jnp.where(hit, lsl3, 0.0)).reshape(1, tt).astype(I32)
    tcnt_ref[0] = tcnt
    tbase_ref[0] = base
    cnt_s[...] = base + tcnt
    cnt_ref[...] = cnt_s[...]


def _route(logits_t, router_b):
    tokens = logits_t.shape[1]
    tt = MOE_TILE
    n_tiles = tokens // tt
    r_i = lax.broadcasted_iota(I32, (tt, tt), 0)
    c_i = lax.broadcasted_iota(I32, (tt, tt), 1)
    tri = (r_i < c_i).astype(BF16)
    row = lambda: pl.BlockSpec((TOP_K, tt), lambda i: (0, i))
    stat = lambda: pl.BlockSpec((1, N_EXPERTS, LANES), lambda i: (i, 0, 0))
    return pl.pallas_call(
        _route_kernel,
        grid=(n_tiles,),
        in_specs=[pl.BlockSpec((N_EXPERTS, tt), lambda i: (0, i)),
                  pl.BlockSpec((N_EXPERTS, 1), lambda i: (0, 0)),
                  pl.BlockSpec((tt, tt), lambda i: (0, 0))],
        out_specs=[row(), row(), stat(), stat(),
                   pl.BlockSpec((N_EXPERTS, LANES), lambda i: (0, 0))],
        out_shape=[jax.ShapeDtypeStruct((TOP_K, tokens), F32),
                   jax.ShapeDtypeStruct((TOP_K, tokens), I32),
                   jax.ShapeDtypeStruct((n_tiles, N_EXPERTS, LANES), F32),
                   jax.ShapeDtypeStruct((n_tiles, N_EXPERTS, LANES), F32),
                   jax.ShapeDtypeStruct((N_EXPERTS, LANES), F32)],
        scratch_shapes=[pltpu.VMEM((N_EXPERTS, LANES), F32)],
        compiler_params=_cparams("arbitrary"),
        name="route",
    )(logits_t, router_b.reshape(N_EXPERTS, 1), tri)


MOE_TILE = 256
ROW_GROUP = 8
SLAB = 32
SORT_CHUNK = 512
FILL_ROWS = 64


def _dispatch_kernel(fill_ref, tab_ref, lslot_ref, h_ref, xs_hbm, xbuf, zbuf, pending, sem, fill_sem):
    tt, dw = h_ref.shape
    i = pl.program_id(0)
    slot = lax.rem(i, 2)
    n_chunks = xbuf.shape[1] // SORT_CHUNK

    @pl.when(i == 0)
    def _zero_fill():
        zbuf[...] = jnp.zeros_like(zbuf)

        def zero_piece(rows, dst):
            return pltpu.make_async_copy(zbuf.at[pl.ds(0, rows)], xs_hbm.at[pl.ds(dst, rows)], fill_sem)

        def run(n, rows, start, do_start):
            def body(s, carry):
                cp = zero_piece(rows, start + pl.multiple_of(s * rows, rows))
                if do_start:
                    cp.start()
                else:
                    cp.wait()
                return carry
            lax.fori_loop(0, n, body, 0)

        def per_expert(e, carry):
            start = pl.multiple_of(fill_ref[0, 0, e], ROW_GROUP)
            n_big = fill_ref[0, 0, N_EXPERTS + e]
            run(n_big, SLAB, start, True)
            run(fill_ref[0, 0, 2 * N_EXPERTS + e], ROW_GROUP,
                start + pl.multiple_of(n_big * SLAB, SLAB), True)
            return carry

        lax.fori_loop(0, N_EXPERTS, per_expert, 0)
        n_tail = fill_ref[0, 0, 3 * N_EXPERTS + 1]
        run(n_tail, FILL_ROWS, pl.multiple_of(fill_ref[0, 0, 3 * N_EXPERTS], FILL_ROWS), True)
        run(fill_ref[0, 0, 3 * N_EXPERTS + 2], SLAB, 0, False)
        run(fill_ref[0, 0, 3 * N_EXPERTS + 3], ROW_GROUP, 0, False)
        run(n_tail, FILL_ROWS, 0, False)

    lo, hi = _unpack_pairs(h_ref[...])
    lo = lo.astype(BF16)
    hi = hi.astype(BF16)
    lslot = lslot_ref[...].astype(jnp.int16)
    for c in range(n_chunks):
        r_i = lax.broadcasted_iota(jnp.int16, (SORT_CHUNK, tt), 0) + jnp.int16(c * SORT_CHUNK)
        perm = jnp.zeros((SORT_CHUNK, tt), BF16)
        for k in range(TOP_K):
            perm = jnp.where(r_i == lslot[k:k + 1, :], jnp.bfloat16(1.0), perm)
        x_lo = jnp.dot(perm, lo, preferred_element_type=F32)
        x_hi = jnp.dot(perm, hi, preferred_element_type=F32)
        xbuf[slot, c * SORT_CHUNK:(c + 1) * SORT_CHUNK, :] = _pack_pairs(
            jnp.concatenate([x_lo, x_hi], axis=-1))

    def piece(rows, sl, src, dst):
        return pltpu.make_async_copy(xbuf.at[sl, pl.ds(src, rows)], xs_hbm.at[pl.ds(dst, rows)], sem)

    def drain(n_big, n_small):
        def big(_, carry):
            piece(SLAB, 0, 0, 0).wait()
            return carry

        def small(_, carry):
            piece(ROW_GROUP, 0, 0, 0).wait()
            return carry

        lax.fori_loop(0, n_big, big, 0)
        lax.fori_loop(0, n_small, small, 0)

    @pl.when(i > 0)
    def _wait_previous():
        drain(pending[0], pending[1])

    def per_expert(e, carry):
        dst = pl.multiple_of(tab_ref[0, 0, e], ROW_GROUP)
        src = pl.multiple_of(tab_ref[0, 0, N_EXPERTS + e], ROW_GROUP)
        n_big = tab_ref[0, 0, 2 * N_EXPERTS + e]
        n_small = tab_ref[0, 0, 3 * N_EXPERTS + e]

        def big(s, c2):
            off = pl.multiple_of(s * SLAB, SLAB)
            piece(SLAB, slot, src + off, dst + off).start()
            return c2

        def small(s, c2):
            off = pl.multiple_of(n_big * SLAB + s * ROW_GROUP, ROW_GROUP)
            piece(ROW_GROUP, slot, src + off, dst + off).start()
            return c2

        lax.fori_loop(0, n_big, big, 0)
        lax.fori_loop(0, n_small, small, 0)
        return carry

    lax.fori_loop(0, N_EXPERTS, per_expert, 0)
    pending[0] = tab_ref[0, 0, 4 * N_EXPERTS]
    pending[1] = tab_ref[0, 0, 4 * N_EXPERTS + 1]

    @pl.when(i == pl.num_programs(0) - 1)
    def _wait_last():
        drain(pending[0], pending[1])


def _dispatch(fill, tab, lslot, h2p_flat, rows_total):
    tokens, dw = h2p_flat.shape
    tt = MOE_TILE
    local_rows = pl.cdiv(TOP_K * tt + N_EXPERTS * (ROW_GROUP - 1), SORT_CHUNK) * SORT_CHUNK
    return pl.pallas_call(
        _dispatch_kernel,
        grid=(tokens // tt,),
        in_specs=[pl.BlockSpec(fill.shape, lambda i: (0, 0, 0), memory_space=pltpu.SMEM),
                  pl.BlockSpec((1, 1, tab.shape[2]), lambda i: (i, 0, 0), memory_space=pltpu.SMEM),
                  pl.BlockSpec((TOP_K, tt), lambda i: (0, i)),
                  pl.BlockSpec((tt, dw), lambda i: (i, 0))],
        out_specs=pl.BlockSpec(memory_space=pl.ANY),
        out_shape=jax.ShapeDtypeStruct((rows_total, dw), h2p_flat.dtype),
        scratch_shapes=[pltpu.VMEM((2, local_rows, dw), U32), pltpu.VMEM((FILL_ROWS, dw), U32),
                        pltpu.SMEM((2,), I32), pltpu.SemaphoreType.DMA, pltpu.SemaphoreType.DMA],
        compiler_params=_cparams("arbitrary"),
        name="dispatch",
    )(fill, tab, lslot, h2p_flat)


def _swiglu_packed(xw, w1, w3, w2):
    lo, hi = _unpack_pairs(xw)
    x = jnp.concatenate([lo.astype(BF16), hi.astype(BF16)], axis=-1)
    a = jnp.dot(x, w1, preferred_element_type=F32)
    b = jnp.dot(x, w3, preferred_element_type=F32)
    hmid = (a * jax.nn.sigmoid(a) * b).astype(BF16)
    return jnp.dot(hmid, w2, preferred_element_type=F32)


def _expert_kernel(be_ref, nu_ref, x_hbm, w1_ref, w3_ref, w2_ref, y_ref, w1_b, w3_b, w2_b, xbuf, sem):
    j = pl.program_id(0)
    rb = xbuf.shape[1]
    last = nu_ref[0] - 1

    def fetch(blk, slot):
        row0 = pl.multiple_of(jnp.minimum(blk, last) * rb, rb)
        return pltpu.make_async_copy(x_hbm.at[pl.ds(row0, rb)], xbuf.at[slot], sem.at[slot])

    @pl.when(j == 0)
    def _prime():
        fetch(0, 0).start()
        fetch(1, 1).start()

    fetch(j + 2, lax.rem(j + 2, 3)).start()
    slot = lax.rem(j, 3)
    fetch(j, slot).wait()

    @pl.when(j <= last)
    def _compute():
        @pl.when(jnp.logical_or(j == 0, be_ref[j] != be_ref[jnp.maximum(j - 1, 0)]))
        def _cast_weights():
            w1_b[...] = w1_ref[0].astype(BF16)
            w3_b[...] = w3_ref[0].astype(BF16)
            w2_b[...] = w2_ref[0].astype(BF16)

        y_ref[...] = _pack_pairs(_swiglu_packed(xbuf[slot], w1_b[...], w3_b[...], w2_b[...]))

    @pl.when(j > last)
    def _unused():
        y_ref[...] = jnp.zeros_like(y_ref)

    @pl.when(j == pl.num_programs(0) - 1)
    def _drain():
        fetch(j + 1, lax.rem(j + 1, 3)).wait()
        fetch(j + 2, lax.rem(j + 2, 3)).wait()


def _experts(block_e, n_used, xs, w1, w3, w2, rb):
    rows, dw = xs.shape
    nblk = rows // rb
    d, f = w1.shape[1:]
    return pl.pallas_call(
        _expert_kernel,
        grid_spec=pltpu.PrefetchScalarGridSpec(
            num_scalar_prefetch=2,
            grid=(nblk,),
            in_specs=[pl.BlockSpec(memory_space=pl.ANY),
                      pl.BlockSpec((1, d, f), lambda j, be, nu: (be[j], 0, 0)),
                      pl.BlockSpec((1, d, f), lambda j, be, nu: (be[j], 0, 0)),
                      pl.BlockSpec((1, f, d), lambda j, be, nu: (be[j], 0, 0))],
            out_specs=pl.BlockSpec((rb, dw), lambda j, be, nu: (j, 0)),
            scratch_shapes=[pltpu.VMEM((d, f), BF16), pltpu.VMEM((d, f), BF16), pltpu.VMEM((f, d), BF16),
                            pltpu.VMEM((3, rb, dw), U32), pltpu.SemaphoreType.DMA((3,))],
        ),
        out_shape=jax.ShapeDtypeStruct((rows, dw), U32),
        compiler_params=_cparams("arbitrary"),
        name="experts",
    )(block_e, n_used, xs, w1, w3, w2)


def _combine_kernel(tab0_ref, tabc_ref, tabn_ref, y_hbm, lslot_ref, wts_ref, h2_ref, x_ref, mod_ref,
                    sw1_ref, sw3_ref, sw2_ref, fg_ref, out_ref, ybuf, sem, *, final):
    nb, ts, d = x_ref.shape
    tc = nb * ts
    dw = d // 2
    i = pl.program_id(0)
    slot = lax.rem(i, 2)
    other = 1 - slot
    n_chunks = ybuf.shape[1] // SORT_CHUNK

    def piece(rows, sl, src, dst):
        return pltpu.make_async_copy(y_hbm.at[pl.ds(src, rows)], ybuf.at[sl, pl.ds(dst, rows)], sem.at[sl])

    def fetch(tab, sl):
        def per_expert(e, carry):
            src = pl.multiple_of(tab[0, 0, e], ROW_GROUP)
            dst = pl.multiple_of(tab[0, 0, N_EXPERTS + e], ROW_GROUP)
            n_big = tab[0, 0, 2 * N_EXPERTS + e]
            n_small = tab[0, 0, 3 * N_EXPERTS + e]

            def big(s, c2):
                off = pl.multiple_of(s * SLAB, SLAB)
                piece(SLAB, sl, src + off, dst + off).start()
                return c2

            def small(s, c2):
                off = pl.multiple_of(n_big * SLAB + s * ROW_GROUP, ROW_GROUP)
                piece(ROW_GROUP, sl, src + off, dst + off).start()
                return c2

            lax.fori_loop(0, n_big, big, 0)
            lax.fori_loop(0, n_small, small, 0)
            return carry

        lax.fori_loop(0, N_EXPERTS, per_expert, 0)

    def drain(tab, sl):
        def big(_, carry):
            piece(SLAB, sl, 0, 0).wait()
            return carry

        def small(_, carry):
            piece(ROW_GROUP, sl, 0, 0).wait()
            return carry

        lax.fori_loop(0, tab[0, 0, 4 * N_EXPERTS], big, 0)
        lax.fori_loop(0, tab[0, 0, 4 * N_EXPERTS + 1], small, 0)

    @pl.when(i == 0)
    def _first_tile():
        ybuf[...] = jnp.zeros_like(ybuf)
        fetch(tab0_ref, 0)

    fetch(tabn_ref, other)
    shared = _swiglu_packed(h2_ref[...].reshape(tc, dw), sw1_ref[...], sw3_ref[...], sw2_ref[...])
    drain(tabc_ref, slot)

    lslot = lslot_ref[...].astype(jnp.int16)
    wts = wts_ref[...].astype(BF16)
    acc_lo = shared[:, :dw]
    acc_hi = shared[:, dw:]
    for c in range(n_chunks):
        r_i = lax.broadcasted_iota(jnp.int16, (SORT_CHUNK, tc), 0) + jnp.int16(c * SORT_CHUNK)
        wt = jnp.zeros((SORT_CHUNK, tc), BF16)
        for k in range(TOP_K):
            wt = jnp.where(r_i == lslot[k:k + 1, :], wts[k:k + 1, :], wt)
        wm = wt.astype(F32).T.astype(BF16)
        lo, hi = _unpack_pairs(ybuf[slot, c * SORT_CHUNK:(c + 1) * SORT_CHUNK, :])
        acc_lo = acc_lo + jnp.dot(wm, lo.astype(BF16), preferred_element_type=F32)
        acc_hi = acc_hi + jnp.dot(wm, hi.astype(BF16), preferred_element_type=F32)
    acc = jnp.concatenate([acc_lo, acc_hi], axis=-1)
    xn = x_ref[...] + mod_ref[:, :, 5 * d:6 * d] * acc.reshape(nb, ts, d)
    if final:
        xn = _rms(xn) * fg_ref[...]
    out_ref[...] = xn

    @pl.when(i == pl.num_programs(0) - 1)
    def _last_tile():
        drain(tabn_ref, other)


def _combine(tab, y_sorted, lslot_t, wts_t, h2p, x, mod, sw1, sw3, sw2, final_g, final):
    batch, seq, d = x.shape
    dw = d // 2
    nb, ts = _seq_tile(batch, seq, MOE_TILE)
    tc = nb * ts
    n_si = seq // ts
    n_tiles = (batch // nb) * n_si
    local_rows = pl.cdiv(TOP_K * tc + N_EXPERTS * (ROW_GROUP - 1), SORT_CHUNK) * SORT_CHUNK
    tile = lambda w: pl.BlockSpec((nb, ts, w), lambda i: (i // n_si, i % n_si, 0))
    full = lambda a: pl.BlockSpec(a.shape, lambda i: (0,) * a.ndim)
    tab_spec = lambda f: pl.BlockSpec((1, 1, tab.shape[2]), lambda i: (f(i), 0, 0), memory_space=pltpu.SMEM)
    per_token = lambda: pl.BlockSpec((TOP_K, tc), lambda i: (0, i))
    return pl.pallas_call(
        functools.partial(_combine_kernel, final=final),
        grid=(n_tiles,),
        in_specs=[tab_spec(lambda i: 0), tab_spec(lambda i: i),
                  tab_spec(lambda i: jnp.minimum(i + 1, n_tiles - 1)),
                  pl.BlockSpec(memory_space=pl.ANY), per_token(), per_token(),
                  tile(dw), tile(d), pl.BlockSpec((nb, 1, 6 * d), lambda i: (i // n_si, 0, 0)),
                  full(sw1), full(sw3), full(sw2), full(final_g)],
        out_specs=tile(d),
        out_shape=jax.ShapeDtypeStruct((batch, seq, d), F32),
        scratch_shapes=[pltpu.VMEM((2, local_rows, dw), U32), pltpu.SemaphoreType.DMA((2,))],
        compiler_params=_cparams("arbitrary"),
        name="combine",
    )(tab, tab, tab, y_sorted, lslot_t, wts_t, h2p, x, mod, sw1, sw3, sw2, final_g)


def _expert_block_rows(tokens):
    if tokens >= 16384:
        return 1024
    if tokens >= 2048:
        return 256
    return 64


def _routing_tables(counts, tcnt, tbase, rb, nblk):
    counts = counts.astype(I32)
    padded = (counts + rb - 1) // rb * rb
    pends = jnp.cumsum(padded)
    pstarts = pends - padded
    block_start = jnp.arange(nblk, dtype=I32) * rb
    block_e = jnp.minimum(jnp.sum(pends[None, :] <= block_start[:, None], axis=1), N_EXPERTS - 1).astype(I32)
    n_used = (pends[-1:] // rb).astype(I32)
    tcnt = tcnt.astype(I32)
    first_row = pstarts[None, :] + tbase.astype(I32)
    local_off = jnp.cumsum(tcnt, axis=1) - tcnt
    n_big = tcnt // SLAB
    n_small = (tcnt - n_big * SLAB) // ROW_GROUP
    totals = jnp.stack([jnp.sum(n_big, axis=1), jnp.sum(n_small, axis=1)], axis=1)
    tab = jnp.concatenate([first_row, local_off, n_big, n_small, totals], axis=1)
    tab = jnp.pad(tab, ((0, 0), (0, 5 * N_EXPERTS - tab.shape[1])))[:, None, :]
    pad_rows = padded - counts
    pad_big = pad_rows // SLAB
    pad_small = (pad_rows - pad_big * SLAB) // ROW_GROUP
    tail = jnp.stack([pends[-1], (nblk * rb - pends[-1]) // FILL_ROWS, jnp.sum(pad_big), jnp.sum(pad_small)])
    fill = jnp.concatenate([pstarts + counts, pad_big, pad_small, tail])
    fill = jnp.pad(fill, (0, 4 * N_EXPERTS - fill.shape[0]))[None, None, :]
    return block_e, n_used, tab, fill


def _layer(x, mod, conv0, c0, n0, m0, p, final_g, final):
    batch, seq, d = x.shape
    tokens = batch * seq
    pc, pm, gcol, grow = _inproj(x, mod, p["norm1_g"], p["w_main"], p["wg_col"], p["b_col"])
    grow = grow.reshape(batch, N_GATE, seq // CHUNK, CHUNK).transpose(0, 2, 1, 3)
    m0b = jnp.broadcast_to(jnp.pad(m0, ((0, 0), (0, 8 - ML_H)))[:, :, None], (batch, 8, LANES))
    n0b = jnp.pad(n0, ((0, 0), (0, 8 - ML_H), (0, 0)))
    x1, h2, logits_t, conv_new, c_new, n_new, m_new = _mixer(
        x, mod, pc, pm, gcol, grow, conv0, c0, n0b, m0b, p["conv_w"], p["conv_norm_g"],
        p["mlstm_norm_g"], p["w_out"], p["norm2_g"], p["router_wt"])
    logits_t = logits_t.transpose(1, 0, 2).reshape(N_EXPERTS, tokens)
    wts, lslot, tcnt, tbase, cnt = _route(logits_t, p["router_b"])
    rb = _expert_block_rows(tokens)
    n_tiles = tokens // MOE_TILE
    nblk = pl.cdiv(tokens * TOP_K + n_tiles * N_EXPERTS * (ROW_GROUP - 1), rb) + N_EXPERTS
    block_e, n_used, tab, fill = _routing_tables(cnt[:, 0], tcnt[:, :, 0], tbase[:, :, 0], rb, nblk)
    xs = _dispatch(fill, tab, lslot, h2.reshape(tokens, d // 2), nblk * rb)
    ys = _experts(block_e, n_used, xs, p["exp_w1"], p["exp_w3"], p["exp_w2"], rb)
    out = _combine(tab, ys, lslot, wts, h2, x1, mod, p["shared_w1"], p["shared_w3"], p["shared_w2"],
                   final_g, final)
    return out, conv_new, c_new, n_new[:, :ML_H, :], m_new[:, :ML_H, 0]


def _trunk(x, mods, conv0, c0, n0, m0, layers, final_g):
    depth = len(layers)
    convs, cs, ns, ms = [], [], [], []
    for l in range(depth):
        x, cv, c, n, m = _layer(x, mods[l], conv0[l], c0[l], n0[l], m0[l], layers[l], final_g,
                                final=(l == depth - 1))
        convs.append(cv); cs.append(c); ns.append(n); ms.append(m)
    return x, jnp.stack(convs), jnp.stack(cs), jnp.stack(ns), jnp.stack(ms)


def kernel(x_prompt, x_sample, c_prompt, c_sample, cache_conv, state_mlstm_C, state_mlstm_n, state_mlstm_m,
           norm1_g, ada_w, ada_b, w_in, b_igate, b_fgate, conv_w, conv_norm_g, mlstm_norm_g, w_out, norm2_g,
           router_w, router_b, exp_w1, exp_w3, exp_w2, shared_w1, shared_w3, shared_w2, final_g):
    depth = w_in.shape[0]
    bp = x_prompt.shape[0]
    d = x_prompt.shape[-1]
    mods = _ada(jnp.concatenate([c_prompt, c_sample], axis=0), ada_w, ada_b)
    mods_p = mods[:, :bp, None, :]
    mods_s = mods[:, bp:, None, :]

    w_gate = w_in[:, :, N_MAIN:]
    bias = jnp.concatenate([b_igate, b_fgate], axis=-1)
    layers = []
    for l in range(depth):
        layers.append(dict(
            norm1_g=norm1_g[l][None, :],
            w_main=w_in[l, :, :N_MAIN].astype(BF16),
            wg_col=jnp.pad(w_gate[l], ((0, 0), (0, LANES - N_GATE))),
            b_col=jnp.pad(bias[l], (0, LANES - N_GATE))[None, :],
            conv_w=conv_w[l], conv_norm_g=conv_norm_g[l][None, :], mlstm_norm_g=mlstm_norm_g[l][None, :],
            w_out=w_out[l].astype(BF16), norm2_g=norm2_g[l][None, :],
            router_wt=router_w[l].T, router_b=router_b[l],
            exp_w1=exp_w1[l], exp_w3=exp_w3[l], exp_w2=exp_w2[l],
            shared_w1=shared_w1[l].astype(BF16), shared_w3=shared_w3[l].astype(BF16),
            shared_w2=shared_w2[l].astype(BF16)))
    fg = final_g[None, :]

    zeros = lambda *s: jnp.zeros((depth, bp) + s, x_prompt.dtype)
    y_p, conv_p, c_p, n_p, m_p = _trunk(
        x_prompt, mods_p, zeros(CONV_K - 1, CONV_W), zeros(ML_H, ML_DK, ML_DK), zeros(ML_H, ML_DK),
        zeros(ML_H), layers, fg)
    y_s, conv_s, c_s, n_s, m_s = _trunk(
        x_sample, mods_s, cache_conv, state_mlstm_C, state_mlstm_n, state_mlstm_m, layers, fg)
    return (y_p, y_s, conv_p, c_p, n_p, m_p, conv_s, c_s, n_s, m_s)
```
